```python
import math
import jax, jax.numpy as jnp
from jax import lax
import numpy as np

D_MODEL = 2048
BATCH = 16
SEQ = 2048
DEPTH = 2
DEC_BATCH = 32
DEC_SEQ = 32
PAST_LEN = 2048

CHUNK = 64
Q_BLOCK = 128
HEAD_DIM = 128
SELF_WIDTH = 1536
DIFF_HEADS = SELF_WIDTH // (2 * HEAD_DIM)
DIFF_VDIM = 2 * HEAD_DIM
SB_HEADS = SELF_WIDTH // HEAD_DIM
MEM_HEADS = 4
MEM_WIDTH = MEM_HEADS * HEAD_DIM
MIX_WIDTH = SELF_WIDTH + MEM_WIDTH
N_MEM = 256
IN_WIDTH = 3 * SELF_WIDTH + MEM_WIDTH
N_BUCKETS = 32
MAX_DISTANCE = 128
PEER_HEADS = 8
PEER_QDIM = 256
N_KEYS = 128
N_EXPERTS = N_KEYS * N_KEYS
PEER_TOPK = 16
PEER_BLOCK = 256
N_MIXERS = 2
N_DIFF = (DEPTH + 1) // 2
ALPHA = (2 * DEPTH) ** 0.25
BETA = (8 * DEPTH) ** -0.25
LN_EPS = 1e-5

kernel_name = "hybrid_diffattn_stickbreak_peer_stream_step"


def layer_norm(x, g, b):
    xf = x.astype(jnp.float32)
    mu = jnp.mean(xf, axis=-1, keepdims=True)
    var = jnp.mean(jnp.square(xf - mu), axis=-1, keepdims=True)
    return ((xf - mu) * lax.rsqrt(var + LN_EPS) * g.astype(jnp.float32) + b.astype(jnp.float32)).astype(x.dtype)


def head_rms(o, g):
    of = o.astype(jnp.float32)
    return (of * lax.rsqrt(jnp.mean(jnp.square(of), axis=-1, keepdims=True) + LN_EPS) * g.astype(jnp.float32)).astype(o.dtype)


def t5_bucket(rel):
    nb = N_BUCKETS // 2
    max_exact = nb // 2
    ret = jnp.where(rel > 0, nb, 0)
    n = jnp.abs(rel)
    nf = jnp.maximum(n, 1).astype(jnp.float32)
    large = max_exact + (jnp.log(nf / max_exact) / math.log(MAX_DISTANCE / max_exact) * (nb - max_exact)).astype(jnp.int32)
    large = jnp.minimum(large, nb - 1)
    return ret + jnp.where(n < max_exact, n, large)


def diff_attend_block(q, k, v, q_pos, k_pos, rel_table, lam):
    logits = jnp.einsum('bqhcd,bkhcd->bchqk', q, k, preferred_element_type=jnp.float32) * (HEAD_DIM ** -0.5)
    bias = jnp.transpose(rel_table.astype(jnp.float32)[t5_bucket(k_pos[None, :] - q_pos[:, None])], (2, 0, 1))
    visible = (k_pos[None, :] // CHUNK) <= (q_pos[:, None] // CHUNK)
    logits = jnp.where(visible, logits + bias[None, None], -jnp.inf)
    p = jax.nn.softmax(logits, axis=-1)
    w = p[:, 0] - lam * p[:, 1]
    return jnp.einsum('bhqk,bkhe->bqhe', w.astype(v.dtype), v)


def sb_attend_block(q, k, v, q_pos, k_pos):
    z = jnp.einsum('bqhd,bkhd->bhqk', q, k, preferred_element_type=jnp.float32) * (HEAD_DIM ** -0.5)
    before = k_pos[None, :] < q_pos[:, None]
    log_keep = jnp.where(before, jax.nn.log_sigmoid(-z), 0.0)
    between = lax.cumsum(log_keep, axis=3, reverse=True) - log_keep
    a = jnp.where(before, jnp.exp(jax.nn.log_sigmoid(z) + between), 0.0)
    return jnp.einsum('bhqk,bkhd->bqhd', a.astype(v.dtype), v)


def sweep_queries(fn, q, q_pos):
    B, T = q.shape[:2]
    if T <= Q_BLOCK:
        return fn(q, q_pos)
    nb = T // Q_BLOCK
    qb = jnp.moveaxis(q.reshape((B, nb, Q_BLOCK) + q.shape[2:]), 1, 0)
    pb = q_pos.reshape(nb, Q_BLOCK)
    out = lax.map(lambda a: fn(a[0], a[1]), (qb, pb))
    out = jnp.moveaxis(out, 0, 1)
    return out.reshape((B, T) + out.shape[3:])


def self_mixer(kind, layer, q, k, v, q_pos, k_pos, rel_table, lam_vec, subln_g):
    B, Tq = q.shape[:2]
    Tk = k.shape[1]
    if kind == 0:
        lam_init = 0.8 - 0.6 * math.exp(-0.3 * layer)
        lp = lam_vec.astype(jnp.float32)
        lam = jnp.exp(jnp.sum(lp[0] * lp[1])) - jnp.exp(jnp.sum(lp[2] * lp[3])) + lam_init
        qh = q.reshape(B, Tq, DIFF_HEADS, 2, HEAD_DIM)
        kh = k.reshape(B, Tk, DIFF_HEADS, 2, HEAD_DIM)
        vh = v.reshape(B, Tk, DIFF_HEADS, DIFF_VDIM)
        o = sweep_queries(lambda qb, pb: diff_attend_block(qb, kh, vh, pb, k_pos, rel_table, lam), qh, q_pos)
        o = head_rms(o, subln_g) * (1.0 - lam_init)
    else:
        qh = q.reshape(B, Tq, SB_HEADS, HEAD_DIM)
        kh = k.reshape(B, Tk, SB_HEADS, HEAD_DIM)
        vh = v.reshape(B, Tk, SB_HEADS, HEAD_DIM)
        o = sweep_queries(lambda qb, pb: sb_attend_block(qb, kh, vh, pb, k_pos), qh, q_pos)
    return o.reshape(B, Tq, SELF_WIDTH)


def mem_attend(mq, mk, mv):
    B, T = mq.shape[:2]
    qh = mq.reshape(B, T, MEM_HEADS, HEAD_DIM)
    logits = jnp.einsum('bqhd,bmhd->bhqm', qh, mk, preferred_element_type=jnp.float32) * (HEAD_DIM ** -0.5)
    p = jax.nn.softmax(logits, axis=-1)
    return jnp.einsum('bhqm,bmhd->bqhd', p.astype(mv.dtype), mv).reshape(B, T, MEM_WIDTH)


def peer_ffn(x, w_q, sub_keys, u_tab, v_tab):
    shp = x.shape
    xt = x.reshape(-1, D_MODEL)
    T = xt.shape[0]
    q = (xt @ w_q).reshape(T, PEER_HEADS, 2, PEER_QDIM // 2)
    s = jnp.einsum('thcd,hcnd->thcn', q, sub_keys, preferred_element_type=jnp.float32)
    sv, si = lax.top_k(s, PEER_TOPK)
    cand = (sv[:, :, 0, :, None] + sv[:, :, 1, None, :]).reshape(T, PEER_HEADS, PEER_TOPK * PEER_TOPK)
    cidx = (si[:, :, 0, :, None] * N_KEYS + si[:, :, 1, None, :]).reshape(T, PEER_HEADS, PEER_TOPK * PEER_TOPK)
    best, pos = lax.top_k(cand, PEER_TOPK)
    experts = jnp.take_along_axis(cidx, pos, axis=-1).reshape(T, PEER_HEADS * PEER_TOPK)
    gates = jax.nn.softmax(best, axis=-1).reshape(T, PEER_HEADS * PEER_TOPK)
    n_blk = -(-T // PEER_BLOCK)
    pad = n_blk * PEER_BLOCK - T
    xb = jnp.pad(xt, ((0, pad), (0, 0))).reshape(n_blk, PEER_BLOCK, D_MODEL)
    eb = jnp.pad(experts, ((0, pad), (0, 0))).reshape(n_blk, PEER_BLOCK, PEER_HEADS * PEER_TOPK)
    gb = jnp.pad(gates, ((0, pad), (0, 0))).reshape(n_blk, PEER_BLOCK, PEER_HEADS * PEER_TOPK)

    def block(args):
        xs, es, gs = args
        act = jnp.einsum('td,tkd->tk', xs, u_tab[es], preferred_element_type=jnp.float32)
        h = (jax.nn.gelu(act, approximate=False) * gs).astype(xs.dtype)
        return jnp.einsum('tk,tkd->td', h, v_tab[es])

    out = lax.map(block, (xb, eb, gb)).reshape(n_blk * PEER_BLOCK, D_MODEL)[:T]
    return out.reshape(shp)


def split_proj(proj):
    s = SELF_WIDTH
    return proj[..., :s], proj[..., s:2 * s], proj[..., 2 * s:3 * s], proj[..., 3 * s:]


def setup_inputs(seed: int = 0) -> dict:
    key = jax.random.key(seed)
    ks = jax.random.split(key, 20)

    def nrm(k, shape, scale):
        return jax.random.normal(k, shape, jnp.float32) * scale

    return {
        "x_prompt": nrm(ks[0], (BATCH, SEQ, D_MODEL), 1.0),
        "x_sample": nrm(ks[1], (DEC_BATCH, DEC_SEQ, D_MODEL), 1.0),
        "cache_self_k": nrm(ks[2], (DEPTH, DEC_BATCH, PAST_LEN, SELF_WIDTH), 1.0),
        "cache_self_v": nrm(ks[3], (DEPTH, DEC_BATCH, PAST_LEN, SELF_WIDTH), 1.0),
        "cache_mem_k": nrm(ks[4], (DEPTH, DEC_BATCH, N_MEM, MEM_HEADS, HEAD_DIM), 1.0),
        "cache_mem_v": nrm(ks[5], (DEPTH, DEC_BATCH, N_MEM, MEM_HEADS, HEAD_DIM), 1.0),
        "mem_prompt": nrm(ks[6], (BATCH, N_MEM, D_MODEL), 1.0),
        "w_in": nrm(ks[7], (DEPTH, D_MODEL, IN_WIDTH), D_MODEL ** -0.5),
        "w_o": nrm(ks[8], (DEPTH, MIX_WIDTH, D_MODEL), BETA * MIX_WIDTH ** -0.5),
        "w_mem_k": nrm(ks[9], (DEPTH, D_MODEL, MEM_WIDTH), D_MODEL ** -0.5),
        "w_mem_v": nrm(ks[10], (DEPTH, D_MODEL, MEM_WIDTH), D_MODEL ** -0.5),
        "rel_bias_table": nrm(ks[11], (N_BUCKETS, DIFF_HEADS), 0.5),
        "diff_lambda": nrm(ks[12], (N_DIFF, 4, HEAD_DIM), 0.1),
        "diff_subln_g": 1.0 + nrm(ks[13], (N_DIFF, DIFF_VDIM), 0.02),
        "ln_g": 1.0 + nrm(ks[14], (DEPTH, 2, D_MODEL), 0.02),
        "ln_b": nrm(ks[15], (DEPTH, 2, D_MODEL), 0.02),
        "peer_w_q": nrm(ks[16], (DEPTH, D_MODEL, PEER_HEADS * PEER_QDIM), D_MODEL ** -0.5),
        "peer_sub_keys": nrm(ks[17], (DEPTH, PEER_HEADS, 2, N_KEYS, PEER_QDIM // 2), (PEER_QDIM // 2) ** -0.5),
        "peer_u": nrm(ks[18], (DEPTH, N_EXPERTS, D_MODEL), D_MODEL ** -0.5),
        "peer_v": nrm(ks[19], (DEPTH, N_EXPERTS, D_MODEL), BETA * PEER_HEADS ** -0.5),
    }


def reference(x_prompt, x_sample, cache_self_k, cache_self_v, cache_mem_k, cache_mem_v, mem_prompt,
              w_in, w_o, w_mem_k, w_mem_v, rel_bias_table, diff_lambda, diff_subln_g,
              ln_g, ln_b, peer_w_q, peer_sub_keys, peer_u, peer_v):
    bp, n_prompt = x_prompt.shape[:2]
    n_new = x_sample.shape[1]
    n_past = cache_self_k.shape[2]
    pos_p = jnp.arange(n_prompt, dtype=jnp.int32)
    pos_s = n_past + jnp.arange(n_new, dtype=jnp.int32)
    pos_ks = jnp.arange(n_past + n_new, dtype=jnp.int32)

    hp, hs = x_prompt, x_sample
    new_k_p, new_v_p, new_mk_p, new_mv_p, new_k_s, new_v_s = [], [], [], [], [], []
    for i in range(DEPTH):
        kind = i % N_MIXERS
        j = i // N_MIXERS
        lam_vec = diff_lambda[j] if kind == 0 else None
        subln_g = diff_subln_g[j] if kind == 0 else None

        qp, kp, vp, mqp = split_proj(hp @ w_in[i])
        qs, ks_, vs, mqs = split_proj(hs @ w_in[i])
        mkp = (mem_prompt @ w_mem_k[i]).reshape(bp, N_MEM, MEM_HEADS, HEAD_DIM)
        mvp = (mem_prompt @ w_mem_v[i]).reshape(bp, N_MEM, MEM_HEADS, HEAD_DIM)

        op = self_mixer(kind, i, qp, kp, vp, pos_p, pos_p, rel_bias_table, lam_vec, subln_g)
        k_all = jnp.concatenate([cache_self_k[i], ks_], axis=1)
        v_all = jnp.concatenate([cache_self_v[i], vs], axis=1)
        os_ = self_mixer(kind, i, qs, k_all, v_all, pos_s, pos_ks, rel_bias_table, lam_vec, subln_g)

        mop = mem_attend(mqp, mkp, mvp)
        mos = mem_attend(mqs, cache_mem_k[i], cache_mem_v[i])

        hp = layer_norm(ALPHA * hp + jnp.concatenate([op, mop], axis=-1) @ w_o[i], ln_g[i, 0], ln_b[i, 0])
        hs = layer_norm(ALPHA * hs + jnp.concatenate([os_, mos], axis=-1) @ w_o[i], ln_g[i, 0], ln_b[i, 0])

        hp = layer_norm(ALPHA * hp + peer_ffn(hp, peer_w_q[i], peer_sub_keys[i], peer_u[i], peer_v[i]), ln_g[i, 1], ln_b[i, 1])
        hs = layer_norm(ALPHA * hs + peer_ffn(hs, peer_w_q[i], peer_sub_keys[i], peer_u[i], peer_v[i]), ln_g[i, 1], ln_b[i, 1])

        new_k_p.append(kp)
        new_v_p.append(vp)
        new_mk_p.append(mkp)
        new_mv_p.append(mvp)
        new_k_s.append(ks_)
        new_v_s.append(vs)

    return (hp, hs, jnp.stack(new_k_p), jnp.stack(new_v_p), jnp.stack(new_mk_p), jnp.stack(new_mv_p),
            jnp.stack(new_k_s), jnp.stack(new_v_s))
```

```python
import functools
import math

import jax
import jax.numpy as jnp
from jax import lax
from jax.experimental import pallas as pl
from jax.experimental.pallas import tpu as pltpu

F32 = jnp.float32
BF16 = jnp.bfloat16

HEAD_DIM = 128
SELF_WIDTH = 1536
DIFF_HEADS = 6
DIFF_VDIM = 2 * HEAD_DIM
SB_HEADS = 12
MEM_HEADS = 4
MEM_WIDTH = MEM_HEADS * HEAD_DIM
CHUNK = 64
N_BUCKETS = 32
PEER_HEADS = 8
N_KEYS = 128
PEER_TOPK = 16
LN_EPS = 1e-5
SCALE = HEAD_DIM ** -0.5
NEG = -1e30

V7X_VMEM_LIMIT = 56 * 1024 * 1024


def _params(*sem):
    return pltpu.CompilerParams(dimension_semantics=sem, vmem_limit_bytes=V7X_VMEM_LIMIT)


def _tile(n, t):
    t = min(n, t)
    assert n % t == 0, (n, t)
    return t


def _dot(a, b):
    return jnp.dot(a, b, preferred_element_type=F32)


def _dot_nt(a, b):
    return lax.dot_general(a, b, (((1,), (1,)), ((), ())), preferred_element_type=F32)


def _layer_norm(r, g, b):
    mu = jnp.mean(r, axis=-1, keepdims=True)
    d = r - mu
    var = jnp.mean(d * d, axis=-1, keepdims=True)
    return d * lax.rsqrt(var + LN_EPS) * g + b


def _linear_kernel(x_ref, w_ref, *o_refs, segs, nc):
    x = x_ref[...].astype(BF16)
    for o_ref, (start, width) in zip(o_refs, segs):
        for c in range(0, width, nc):
            o_ref[:, c:c + nc] = _dot(x, w_ref[:, start + c:start + c + nc]).astype(o_ref.dtype)


def _linear(x, w, layer, segs, dtypes, tm):
    M, K = x.shape
    N = w.shape[2]
    tm = _tile(M, tm)
    nc = 512
    assert all(wd % nc == 0 for _, wd in segs)
    return pl.pallas_call(
        functools.partial(_linear_kernel, segs=segs, nc=nc),
        grid=(M // tm,),
        in_specs=[
            pl.BlockSpec((tm, K), lambda i: (i, 0)),
            pl.BlockSpec((None, K, N), lambda i: (layer, 0, 0), pipeline_mode=pl.Buffered(1)),
        ],
        out_specs=[pl.BlockSpec((tm, wd), lambda i: (i, 0)) for _, wd in segs],
        out_shape=[jax.ShapeDtypeStruct((M, wd), dt) for (_, wd), dt in zip(segs, dtypes)],
        compiler_params=_params("parallel"),
        name="linear",
    )(x, w)


def _out_proj_kernel(o_ref, mo_ref, w_ref, h_ref, g_ref, b_ref, y_ref, yt_ref, *, alpha):
    acc = _dot(o_ref[...], w_ref[:SELF_WIDTH, :]) + _dot(mo_ref[...], w_ref[SELF_WIDTH:, :])
    y = _layer_norm(alpha * h_ref[...] + acc, g_ref[...], b_ref[...])
    y_ref[...] = y
    yt_ref[...] = y.T.astype(BF16)


def _out_proj(o, mo, w_o, layer, h, ln_g, ln_b, alpha, tm):
    T, D = h.shape
    tm = _tile(T, tm)
    return pl.pallas_call(
        functools.partial(_out_proj_kernel, alpha=alpha),
        grid=(T // tm,),
        in_specs=[
            pl.BlockSpec((tm, SELF_WIDTH), lambda i: (i, 0)),
            pl.BlockSpec((tm, MEM_WIDTH), lambda i: (i, 0)),
            pl.BlockSpec((None, SELF_WIDTH + MEM_WIDTH, D), lambda i: (layer, 0, 0), pipeline_mode=pl.Buffered(1)),
            pl.BlockSpec((tm, D), lambda i: (i, 0)),
            pl.BlockSpec((None, 1, D), lambda i: (layer, 0, 0)),
            pl.BlockSpec((None, 1, D), lambda i: (layer, 0, 0)),
        ],
        out_specs=[pl.BlockSpec((tm, D), lambda i: (i, 0)), pl.BlockSpec((D, tm), lambda i: (0, i))],
        out_shape=[jax.ShapeDtypeStruct((T, D), F32), jax.ShapeDtypeStruct((D, T), BF16)],
        compiler_params=_params("parallel"),
        name="out_proj_ln",
    )(o, mo, w_o, h, ln_g, ln_b)


def _bias_kernel(q0_ref, k0_ref, tab_ref, o_ref, *, tq, tk):
    h = pl.program_id(0)
    t = pl.program_id(1)
    q_pos = q0_ref[t] + lax.broadcasted_iota(jnp.int32, (tq, tk), 0)
    k_pos = k0_ref[t] + lax.broadcasted_iota(jnp.int32, (tq, tk), 1)
    rel = k_pos - q_pos
    n = jnp.abs(rel)
    large = jnp.full((tq, tk), 8, jnp.int32)
    for thr in (12, 16, 23, 32, 46, 64, 91):
        large = large + (n >= thr).astype(jnp.int32)
    bucket = jnp.where(rel > 0, N_BUCKETS // 2, 0) + jnp.where(n < 8, n, large)
    bias = jnp.zeros((tq, tk), F32)
    for b in range(N_BUCKETS):
        bias = jnp.where(bucket == b, tab_ref[b, h], bias)
    visible = (k_pos // CHUNK) <= (q_pos // CHUNK)
    o_ref[...] = jnp.where(visible, bias, NEG)


def _bias_tiles(rel_table, q0, k0, tq, tk):
    n = q0.shape[0]
    return pl.pallas_call(
        functools.partial(_bias_kernel, tq=tq, tk=tk),
        grid_spec=pltpu.PrefetchScalarGridSpec(
            num_scalar_prefetch=2,
            grid=(DIFF_HEADS, n),
            in_specs=[pl.BlockSpec(memory_space=pltpu.SMEM)],
            out_specs=pl.BlockSpec((None, None, tq, tk), lambda h, t, *_: (h, t, 0, 0)),
        ),
        out_shape=jax.ShapeDtypeStruct((DIFF_HEADS, n, tq, tk), F32),
        compiler_params=_params("parallel", "parallel"),
        name="bias_tiles",
    )(q0, k0, rel_table)


def _diff_step(q, k, v, bias, m_sc, l_sc, acc_sc):
    kb = k.astype(BF16)
    vb = v.astype(BF16)
    for c in range(2):
        s = _dot_nt(q[:, c * HEAD_DIM:(c + 1) * HEAD_DIM], kb[:, c * HEAD_DIM:(c + 1) * HEAD_DIM]) * SCALE + bias
        m_prev = m_sc[c]
        m_new = jnp.maximum(m_prev, jnp.max(s, axis=1, keepdims=True))
        alpha = jnp.exp(m_prev - m_new)
        p = jnp.exp(s - m_new)
        l_sc[c] = alpha * l_sc[c] + jnp.sum(p, axis=1, keepdims=True)
        acc_sc[c] = alpha * acc_sc[c] + _dot(p.astype(BF16), vb)
        m_sc[c] = m_new


def _diff_kernel(*refs, has_new, lam_init):
    if has_new:
        lam_ref, g_ref, q_ref, k_ref, v_ref, bias_ref, kn_ref, vn_ref, biasn_ref, o_ref, m_sc, l_sc, acc_sc = refs
    else:
        lam_ref, g_ref, q_ref, k_ref, v_ref, bias_ref, o_ref, m_sc, l_sc, acc_sc = refs
    qi = pl.program_id(2)
    kk = pl.program_id(3)

    @pl.when(kk == 0)
    def _():
        m_sc[...] = jnp.full(m_sc.shape, NEG, F32)
        l_sc[...] = jnp.zeros(l_sc.shape, F32)
        acc_sc[...] = jnp.zeros(acc_sc.shape, F32)

    if has_new:
        @pl.when(kk == 0)
        def _():
            _diff_step(q_ref[...], kn_ref[...], vn_ref[...], biasn_ref[...], m_sc, l_sc, acc_sc)

        @pl.when(kk > 0)
        def _():
            _diff_step(q_ref[...], k_ref[...], v_ref[...], bias_ref[...], m_sc, l_sc, acc_sc)
    else:
        @pl.when(kk <= qi)
        def _():
            _diff_step(q_ref[...], k_ref[...], v_ref[...], bias_ref[...], m_sc, l_sc, acc_sc)

    @pl.when(kk == pl.num_programs(3) - 1)
    def _():
        lp = lam_ref[...]
        lam = (jnp.exp(jnp.sum(lp[0:1] * lp[1:2], axis=1, keepdims=True))
               - jnp.exp(jnp.sum(lp[2:3] * lp[3:4], axis=1, keepdims=True)) + lam_init)
        o = acc_sc[0] / l_sc[0] - lam * (acc_sc[1] / l_sc[1])
        o = o * lax.rsqrt(jnp.mean(o * o, axis=1, keepdims=True) + LN_EPS) * g_ref[...]
        o_ref[...] = (o * (1.0 - lam_init)).astype(o_ref.dtype)


def _diff_attention(q, k, v, bias, lam_vec, subln_g, layer, tq, tk, new=None):
    B, Tq, _ = q.shape
    lam_init = 0.8 - 0.6 * math.exp(-0.3 * layer)
    has_new = new is not None
    if has_new:
        nkc = k.shape[2] // tk
        nk = nkc + 1
        kv_spec = pl.BlockSpec((None, None, tk, DIFF_VDIM),
                               lambda b, h, qi, kk: (layer, b, jnp.clip(nkc - kk, 0, nkc - 1), h))
        bias_spec = pl.BlockSpec((None, None, tq, tk), lambda b, h, qi, kk: (h, jnp.clip(nkc - kk, 0, nkc - 1), 0, 0))
        tn = new[0].shape[1]
        new_specs = [pl.BlockSpec((None, tn, DIFF_VDIM), lambda b, h, qi, kk: (b, 0, h)),
                     pl.BlockSpec((None, tn, DIFF_VDIM), lambda b, h, qi, kk: (b, 0, h)),
                     pl.BlockSpec((None, None, tq, tn), lambda b, h, qi, kk: (h, 0, 0, 0))]
        new_args = list(new)
    else:
        nk = k.shape[1] // tk
        kv_spec = pl.BlockSpec((None, tk, DIFF_VDIM), lambda b, h, qi, kk: (b, jnp.maximum(qi - kk, 0), h))
        bias_spec = pl.BlockSpec((None, None, tq, tk), lambda b, h, qi, kk: (h, jnp.minimum(kk, qi), 0, 0))
        new_specs, new_args = [], []
    return pl.pallas_call(
        functools.partial(_diff_kernel, has_new=has_new, lam_init=lam_init),
        grid=(B, DIFF_HEADS, Tq // tq, nk),
        in_specs=[
            pl.BlockSpec((4, HEAD_DIM), lambda b, h, qi, kk: (0, 0)),
            pl.BlockSpec((1, DIFF_VDIM), lambda b, h, qi, kk: (0, 0)),
            pl.BlockSpec((None, tq, DIFF_VDIM), lambda b, h, qi, kk: (b, qi, h)),
            kv_spec, kv_spec, bias_spec, *new_specs,
        ],
        out_specs=pl.BlockSpec((None, tq, DIFF_VDIM), lambda b, h, qi, kk: (b, qi, h)),
        out_shape=jax.ShapeDtypeStruct((B, Tq, SELF_WIDTH), BF16),
        scratch_shapes=[pltpu.VMEM((2, tq, 1), F32), pltpu.VMEM((2, tq, 1), F32),
                        pltpu.VMEM((2, tq, DIFF_VDIM), F32)],
        compiler_params=_params("parallel", "parallel", "parallel", "arbitrary"),
        name="diff_attention",
    )(lam_vec, subln_g, q, k, v, bias, *new_args)


def _sb_step(q, k, v, q_pos0, k_pos0, acc_sc, r_sc, cw):
    tq = q.shape[0]
    tk = k.shape[0]
    cw = min(cw, tk)
    z = _dot_nt(q, k.astype(BF16)) * SCALE
    vb = v.astype(BF16)
    q_pos = q_pos0 + lax.broadcasted_iota(jnp.int32, (tq, cw), 0)
    later = (lax.broadcasted_iota(jnp.int32, (cw, cw), 0) > lax.broadcasted_iota(jnp.int32, (cw, cw), 1))
    later = jnp.where(later, 1.0, 0.0).astype(BF16)
    for c0 in range(tk - cw, -1, -cw):
        zc = z[:, c0:c0 + cw]
        before = (k_pos0 + c0 + lax.broadcasted_iota(jnp.int32, (tq, cw), 1)) < q_pos
        l1p = jnp.log1p(jnp.exp(-jnp.abs(zc)))
        log_keep = jnp.where(before, -(jnp.maximum(zc, 0.0) + l1p), 0.0)
        log_beta = jnp.minimum(zc, 0.0) - l1p
        hi = log_keep.astype(BF16)
        lo = (log_keep - hi.astype(F32)).astype(BF16)
        between = _dot(hi, later) + _dot(lo, later) + r_sc[...]
        a = jnp.where(before, jnp.exp(log_beta + between), 0.0)
        acc_sc[...] += _dot(a.astype(BF16), vb[c0:c0 + cw])
        r_sc[...] += jnp.sum(log_keep, axis=1, keepdims=True)


def _sb_kernel(*refs, has_new, tq, tk, q_off, nkc, cw):
    if has_new:
        q_ref, k_ref, v_ref, kn_ref, vn_ref, o_ref, acc_sc, r_sc = refs
    else:
        q_ref, k_ref, v_ref, o_ref, acc_sc, r_sc = refs
    qi = pl.program_id(2)
    kk = pl.program_id(3)

    @pl.when(kk == 0)
    def _():
        acc_sc[...] = jnp.zeros(acc_sc.shape, F32)
        r_sc[...] = jnp.zeros(r_sc.shape, F32)

    if has_new:
        @pl.when(kk == 0)
        def _():
            _sb_step(q_ref[...], kn_ref[...], vn_ref[...], q_off, q_off, acc_sc, r_sc, cw)

        @pl.when(kk > 0)
        def _():
            _sb_step(q_ref[...], k_ref[...], v_ref[...], q_off, (nkc - kk) * tk, acc_sc, r_sc, cw)
    else:
        @pl.when(kk <= qi)
        def _():
            _sb_step(q_ref[...], k_ref[...], v_ref[...], qi * tq, (qi - kk) * tk, acc_sc, r_sc, cw)

    @pl.when(kk == pl.num_programs(3) - 1)
    def _():
        o_ref[...] = acc_sc[...].astype(o_ref.dtype)


def _sb_attention(q, k, v, layer, tq, tk, new=None):
    B, Tq, _ = q.shape
    has_new = new is not None
    if has_new:
        nkc = k.shape[2] // tk
        nk = nkc + 1
        q_off = k.shape[2]
        kv_spec = pl.BlockSpec((None, None, tk, HEAD_DIM),
                               lambda b, h, qi, kk: (layer, b, jnp.clip(nkc - kk, 0, nkc - 1), h))
        tn = new[0].shape[1]
        new_specs = [pl.BlockSpec((None, tn, HEAD_DIM), lambda b, h, qi, kk: (b, 0, h))] * 2
        new_args = list(new)
    else:
        assert tq == tk
        nkc = k.shape[1] // tk
        nk = nkc
        q_off = 0
        kv_spec = pl.BlockSpec((None, tk, HEAD_DIM), lambda b, h, qi, kk: (b, jnp.maximum(qi - kk, 0), h))
        new_specs, new_args = [], []
    return pl.pallas_call(
        functools.partial(_sb_kernel, has_new=has_new, tq=tq, tk=tk, q_off=q_off, nkc=nkc, cw=256),
        grid=(B, SB_HEADS, Tq // tq, nk),
        in_specs=[pl.BlockSpec((None, tq, HEAD_DIM), lambda b, h, qi, kk: (b, qi, h)), kv_spec, kv_spec, *new_specs],
        out_specs=pl.BlockSpec((None, tq, HEAD_DIM), lambda b, h, qi, kk: (b, qi, h)),
        out_shape=jax.ShapeDtypeStruct((B, Tq, SELF_WIDTH), BF16),
        scratch_shapes=[pltpu.VMEM((tq, HEAD_DIM), F32), pltpu.VMEM((tq, 1), F32)],
        compiler_params=_params("parallel", "parallel", "parallel", "arbitrary"),
        name="sb_attention",
    )(q, k, v, *new_args)


def _mem_kernel(q_ref, k_ref, v_ref, o_ref):
    q = q_ref[...]
    kb = k_ref[...].astype(BF16)
    vb = v_ref[...].astype(BF16)
    for h in range(MEM_HEADS):
        sl = slice(h * HEAD_DIM, (h + 1) * HEAD_DIM)
        s = _dot_nt(q[:, sl], kb[:, sl]) * SCALE
        p = jnp.exp(s - jnp.max(s, axis=1, keepdims=True))
        o = _dot(p.astype(BF16), vb[:, sl]) / jnp.sum(p, axis=1, keepdims=True)
        o_ref[:, sl] = o.astype(o_ref.dtype)


def _mem_attention(mq, mk, mv, tq, layer=None):
    B, Tq, _ = mq.shape
    n_mem = mk.shape[-2]
    if layer is None:
        kv_spec = pl.BlockSpec((None, n_mem, MEM_WIDTH), lambda b, qi: (b, 0, 0))
    else:
        kv_spec = pl.BlockSpec((None, None, n_mem, MEM_WIDTH), lambda b, qi: (layer, b, 0, 0))
    return pl.pallas_call(
        _mem_kernel,
        grid=(B, Tq // tq),
        in_specs=[pl.BlockSpec((None, tq, MEM_WIDTH), lambda b, qi: (b, qi, 0)), kv_spec, kv_spec],
        out_specs=pl.BlockSpec((None, tq, MEM_WIDTH), lambda b, qi: (b, qi, 0)),
        out_shape=jax.ShapeDtypeStruct((B, Tq, MEM_WIDTH), BF16),
        compiler_params=_params("parallel", "parallel"),
        name="mem_attention",
    )(mq, mk, mv)


def _top_values(x, n):
    out = []
    for _ in range(n):
        m = jnp.max(x, axis=0, keepdims=True)
        out.append(m)
        x = jnp.where(x == m, -jnp.inf, x)
    return out


def _peer_select_kernel(q_ref, keys_ref, thr_ref, e1z_ref, s2_ref, e2_ref, cand_sc):
    n_top = PEER_TOPK + 1
    pairs = [(p, q) for p in range(n_top) for q in range(n_top) if (p + 1) * (q + 1) <= n_top]
    for h in range(PEER_HEADS):
        s = []
        for c in range(2):
            col = (2 * h + c) * HEAD_DIM
            s.append(_dot_nt(keys_ref[h, c], q_ref[:, col:col + HEAD_DIM]))
        a = _top_values(s[0], n_top)
        b = _top_values(s[1], n_top)
        cand_sc[...] = jnp.full(cand_sc.shape, -jnp.inf, F32)
        for r, (p, q) in enumerate(pairs):
            cand_sc[r:r + 1, :] = a[p] + b[q]
        c = _top_values(cand_sc[...], n_top)
        z = jnp.ones_like(c[0])
        for kth in range(1, PEER_TOPK):
            z = z + jnp.exp(c[kth] - c[0])
        tau = 0.5 * (c[PEER_TOPK - 1] + c[PEER_TOPK])
        thr_ref[h] = tau - s[0]
        e1z_ref[h] = jnp.exp(s[0] - a[0]) / z
        s2_ref[h] = s[1]
        e2_ref[h] = jnp.exp(s[1] - b[0])


def _peer_select(qp, keys, layer, tt):
    T = qp.shape[0]
    tt = _tile(T, tt)
    n_pairs = sum(1 for p in range(1, PEER_TOPK + 2) for q in range(1, PEER_TOPK + 2) if p * q <= PEER_TOPK + 1)
    out = jax.ShapeDtypeStruct((PEER_HEADS, N_KEYS, T), F32)
    spec = pl.BlockSpec((PEER_HEADS, N_KEYS, tt), lambda i: (0, 0, i))
    return pl.pallas_call(
        _peer_select_kernel,
        grid=(T // tt,),
        in_specs=[pl.BlockSpec((tt, PEER_HEADS * 2 * HEAD_DIM), lambda i: (i, 0)),
                  pl.BlockSpec((None, PEER_HEADS, 2, N_KEYS, HEAD_DIM), lambda i: (layer, 0, 0, 0, 0))],
        out_specs=[spec] * 4,
        out_shape=[out] * 4,
        scratch_shapes=[pltpu.VMEM((-(-n_pairs // 8) * 8, tt), F32)],
        compiler_params=_params("parallel"),
        name="peer_select",
    )(qp, keys)


def _peer_mix_kernel(xt_ref, u_ref, vt_ref, thr_ref, e1z_ref, s2_ref, e2_ref, h_ref, g_ref, b_ref,
                     y_ref, acc_sc, hid_sc, *, rows, alpha):
    e = pl.program_id(1)

    @pl.when(e == 0)
    def _():
        acc_sc[...] = jnp.zeros(acc_sc.shape, F32)

    act = _dot(u_ref[...], xt_ref[...])
    row0 = (e * rows) % 8
    for ii in range(rows):
        gate = jnp.zeros((N_KEYS, act.shape[1]), F32)
        for h in range(PEER_HEADS):
            thr = thr_ref[h, pl.ds(row0 + ii, 1), :]
            e1z = e1z_ref[h, pl.ds(row0 + ii, 1), :]
            gate = gate + jnp.where(s2_ref[h] >= thr, e2_ref[h] * e1z, 0.0)
        a = act[ii * N_KEYS:(ii + 1) * N_KEYS]
        gelu = 0.5 * a * (1.0 + lax.erf(a * (2.0 ** -0.5)))
        hid_sc[ii * N_KEYS:(ii + 1) * N_KEYS, :] = (gelu * gate).astype(BF16)
    acc_sc[...] += _dot(vt_ref[...], hid_sc[...])

    @pl.when(e == pl.num_programs(1) - 1)
    def _():
        y_ref[...] = _layer_norm(alpha * h_ref[...] + acc_sc[...].T, g_ref[...], b_ref[...])


def _peer_mix(xt, u, vt, sel, layer, h, ln_g, ln_b, alpha, tt, rows):
    D, T = xt.shape
    tt = _tile(T, tt)
    n_exp = u.shape[1]
    eb = rows * N_KEYS
    assert 8 % rows == 0 and n_exp % eb == 0
    row_spec = pl.BlockSpec((PEER_HEADS, 8, tt), lambda t, e: (0, (e * rows) // 8, t))
    full_spec = pl.BlockSpec((PEER_HEADS, N_KEYS, tt), lambda t, e: (0, 0, t))
    thr, e1z, s2, e2 = sel
    return pl.pallas_call(
        functools.partial(_peer_mix_kernel, rows=rows, alpha=alpha),
        grid=(T // tt, n_exp // eb),
        in_specs=[
            pl.BlockSpec((D, tt), lambda t, e: (0, t)),
            pl.BlockSpec((None, eb, D), lambda t, e: (layer, e, 0)),
            pl.BlockSpec((None, D, eb), lambda t, e: (layer, 0, e)),
            row_spec, row_spec, full_spec, full_spec,
            pl.BlockSpec((tt, D), lambda t, e: (t, 0)),
            pl.BlockSpec((None, 1, D), lambda t, e: (layer, 0, 0)),
            pl.BlockSpec((None, 1, D), lambda t, e: (layer, 0, 0)),
        ],
        out_specs=pl.BlockSpec((tt, D), lambda t, e: (t, 0)),
        out_shape=jax.ShapeDtypeStruct((T, D), F32),
        scratch_shapes=[pltpu.VMEM((D, tt), F32), pltpu.VMEM((eb, tt), BF16)],
        compiler_params=_params("parallel", "arbitrary"),
        name="peer_mix",
    )(xt, u, vt, thr, e1z, s2, e2, h, ln_g, ln_b)


def kernel(x_prompt, x_sample, cache_self_k, cache_self_v, cache_mem_k, cache_mem_v, mem_prompt, w_in, w_o, w_mem_k, w_mem_v, rel_bias_table, diff_lambda, diff_subln_g, ln_g, ln_b, peer_w_q, peer_sub_keys, peer_u, peer_v):
    depth = w_in.shape[0]
    bp, n_prompt, d_model = x_prompt.shape
    bs, n_new, _ = x_sample.shape
    n_past = cache_self_k.shape[2]
    n_mem = mem_prompt.shape[1]
    alpha = (2 * depth) ** 0.25
    S = SELF_WIDTH

    w_in_b = w_in.astype(BF16)
    w_o_b = w_o.astype(BF16)
    w_mk_b = w_mem_k.astype(BF16)
    w_mv_b = w_mem_v.astype(BF16)
    w_pq_b = peer_w_q.astype(BF16)
    keys_b = peer_sub_keys.astype(BF16)
    u_b = peer_u.astype(BF16)
    vt_b = jnp.swapaxes(peer_v, 1, 2).astype(BF16)
    ln_g4 = ln_g.reshape(depth, 2, 1, d_model)
    ln_b4 = ln_b.reshape(depth, 2, 1, d_model)
    cmk = cache_mem_k.reshape(depth, bs, n_mem, MEM_WIDTH)
    cmv = cache_mem_v.reshape(depth, bs, n_mem, MEM_WIDTH)
    mem2d = mem_prompt.reshape(bp * n_mem, d_model)

    tq_p = _tile(n_prompt, 512)
    tk_s = _tile(n_past, 512)
    nq_p = n_prompt // tq_p
    nk_s = n_past // tk_s
    i32 = jnp.int32
    bias_p = _bias_tiles(rel_bias_table, jnp.arange(nq_p, dtype=i32) * tq_p, jnp.zeros((nq_p,), i32), tq_p, tq_p)
    bias_sc = _bias_tiles(rel_bias_table, jnp.full((nk_s,), n_past, i32), jnp.arange(nk_s, dtype=i32) * tk_s, n_new, tk_s)
    bias_sn = _bias_tiles(rel_bias_table, jnp.full((1,), n_past, i32), jnp.full((1,), n_past, i32), n_new, n_new)

    hp = x_prompt.reshape(bp * n_prompt, d_model)
    hs = x_sample.reshape(bs * n_new, d_model)
    segs = ((0, S), (S, S), (2 * S, S), (3 * S, MEM_WIDTH))
    seg_dt = (BF16, F32, F32, BF16)
    new_k_p, new_v_p, new_mk_p, new_mv_p, new_k_s, new_v_s = [], [], [], [], [], []
    for i in range(depth):
        kind = i % 2
        j = i // 2
        qp, kp, vp, mqp = _linear(hp, w_in_b, i, segs, seg_dt, 256)
        qs, ks, vs, mqs = _linear(hs, w_in_b, i, segs, seg_dt, 256)
        (mkp,) = _linear(mem2d, w_mk_b, i, ((0, MEM_WIDTH),), (F32,), 512)
        (mvp,) = _linear(mem2d, w_mv_b, i, ((0, MEM_WIDTH),), (F32,), 512)
        qp3, kp3, vp3 = (a.reshape(bp, n_prompt, S) for a in (qp, kp, vp))
        qs3, ks3, vs3 = (a.reshape(bs, n_new, S) for a in (qs, ks, vs))

        if kind == 0:
            g = diff_subln_g[j].reshape(1, DIFF_VDIM)
            op = _diff_attention(qp3, kp3, vp3, bias_p, diff_lambda[j], g, i, tq_p, tq_p)
            os_ = _diff_attention(qs3, cache_self_k, cache_self_v, bias_sc, diff_lambda[j], g, i, n_new, tk_s,
                                  new=(ks3, vs3, bias_sn))
        else:
            op = _sb_attention(qp3, kp3, vp3, i, tq_p, tq_p)
            os_ = _sb_attention(qs3, cache_self_k, cache_self_v, i, n_new, tk_s, new=(ks3, vs3))

        mop = _mem_attention(mqp.reshape(bp, n_prompt, MEM_WIDTH), mkp.reshape(bp, n_mem, MEM_WIDTH),
                             mvp.reshape(bp, n_mem, MEM_WIDTH), tq_p)
        mos = _mem_attention(mqs.reshape(bs, n_new, MEM_WIDTH), cmk, cmv, n_new, layer=i)

        outs = []
        for h_res, o_self, o_mem in ((hp, op, mop), (hs, os_, mos)):
            T = h_res.shape[0]
            y, yt = _out_proj(o_self.reshape(T, S), o_mem.reshape(T, MEM_WIDTH), w_o_b, i, h_res,
                              ln_g4[:, 0], ln_b4[:, 0], alpha, 256)
            (pq,) = _linear(y, w_pq_b, i, ((0, PEER_HEADS * 2 * HEAD_DIM),), (BF16,), 256)
            sel = _peer_select(pq, keys_b, i, 512)
            outs.append(_peer_mix(yt, u_b, vt_b, sel, i, y, ln_g4[:, 1], ln_b4[:, 1], alpha, 512, 4))
        hp, hs = outs

        new_k_p.append(kp3)
        new_v_p.append(vp3)
        new_mk_p.append(mkp.reshape(bp, n_mem, MEM_HEADS, HEAD_DIM))
        new_mv_p.append(mvp.reshape(bp, n_mem, MEM_HEADS, HEAD_DIM))
        new_k_s.append(ks3)
        new_v_s.append(vs3)

    return (hp.reshape(bp, n_prompt, d_model), hs.reshape(bs, n_new, d_model),
            jnp.stack(new_k_p), jnp.stack(new_v_p), jnp.stack(new_mk_p), jnp.stack(new_mv_p),
            jnp.stack(new_k_s), jnp.stack(new_v_s))
```

```python
import functools
import math

import jax
import jax.numpy as jnp
from jax import lax
from jax.experimental import pallas as pl
from jax.experimental.pallas import tpu as pltpu

F32 = jnp.float32
BF16 = jnp.bfloat16

HEAD_DIM = 128
SELF_WIDTH = 1536
DIFF_HEADS = 6
DIFF_VDIM = 2 * HEAD_DIM
SB_HEADS = 12
MEM_HEADS = 4
MEM_WIDTH = MEM_HEADS * HEAD_DIM
CHUNK = 64
N_BUCKETS = 32
PEER_HEADS = 8
N_KEYS = 128
PEER_TOPK = 16
LN_EPS = 1e-5
SCALE = HEAD_DIM ** -0.5
NEG = -1e30

V7X_VMEM_LIMIT = 56 * 1024 * 1024
LANES = 128
MXU_TILE = 256
PEER_MM_ROWS = 256
GATE_SLAB = 32


def _params(*sem):
    return pltpu.CompilerParams(dimension_semantics=sem, vmem_limit_bytes=V7X_VMEM_LIMIT)


def _tile(n, t):
    t = min(n, t)
    assert n % t == 0, (n, t)
    return t


def _dot(a, b):
    return jnp.dot(a, b, preferred_element_type=F32)


def _dot_nt(a, b):
    return lax.dot_general(a, b, (((1,), (1,)), ((), ())), preferred_element_type=F32)


def _layer_norm(r, g, b):
    mu = jnp.mean(r, axis=-1, keepdims=True)
    d = r - mu
    var = jnp.mean(d * d, axis=-1, keepdims=True)
    return d * lax.rsqrt(var + LN_EPS) * g + b


def _linear_kernel(x_ref, w_ref, *o_refs, segs, nc):
    x = x_ref[...].astype(BF16)
    for o_ref, (start, width) in zip(o_refs, segs):
        for c in range(0, width, nc):
            o_ref[:, c:c + nc] = _dot(x, w_ref[:, start + c:start + c + nc]).astype(o_ref.dtype)


def _linear(x, w, layer, segs, dtypes, tm):
    M, K = x.shape
    N = w.shape[2]
    tm = _tile(M, tm)
    nc = 512
    assert all(wd % nc == 0 for _, wd in segs)
    return pl.pallas_call(
        functools.partial(_linear_kernel, segs=segs, nc=nc),
        grid=(M // tm,),
        in_specs=[
            pl.BlockSpec((tm, K), lambda i: (i, 0)),
            pl.BlockSpec((None, K, N), lambda i: (layer, 0, 0), pipeline_mode=pl.Buffered(1)),
        ],
        out_specs=[pl.BlockSpec((tm, wd), lambda i: (i, 0)) for _, wd in segs],
        out_shape=[jax.ShapeDtypeStruct((M, wd), dt) for (_, wd), dt in zip(segs, dtypes)],
        compiler_params=_params("parallel"),
        name="linear",
    )(x, w)


def _out_proj_kernel(o_ref, mo_ref, w_ref, h_ref, g_ref, b_ref, y_ref, yt_ref, *, alpha):
    acc = _dot(o_ref[...], w_ref[:SELF_WIDTH, :]) + _dot(mo_ref[...], w_ref[SELF_WIDTH:, :])
    y = _layer_norm(alpha * h_ref[...] + acc, g_ref[...], b_ref[...])
    y_ref[...] = y
    yt_ref[...] = y.T.astype(BF16)


def _out_proj(o, mo, w_o, layer, h, ln_g, ln_b, alpha, tm):
    T, D = h.shape
    tm = _tile(T, tm)
    return pl.pallas_call(
        functools.partial(_out_proj_kernel, alpha=alpha),
        grid=(T // tm,),
        in_specs=[
            pl.BlockSpec((tm, SELF_WIDTH), lambda i: (i, 0)),
            pl.BlockSpec((tm, MEM_WIDTH), lambda i: (i, 0)),
            pl.BlockSpec((None, SELF_WIDTH + MEM_WIDTH, D), lambda i: (layer, 0, 0), pipeline_mode=pl.Buffered(1)),
            pl.BlockSpec((tm, D), lambda i: (i, 0)),
            pl.BlockSpec((None, 1, D), lambda i: (layer, 0, 0)),
            pl.BlockSpec((None, 1, D), lambda i: (layer, 0, 0)),
        ],
        out_specs=[pl.BlockSpec((tm, D), lambda i: (i, 0)), pl.BlockSpec((D, tm), lambda i: (0, i))],
        out_shape=[jax.ShapeDtypeStruct((T, D), F32), jax.ShapeDtypeStruct((D, T), BF16)],
        compiler_params=_params("parallel"),
        name="out_proj_ln",
    )(o, mo, w_o, h, ln_g, ln_b)


def _bias_kernel(q0_ref, k0_ref, tab_ref, o_ref, *, tq, tk):
    h = pl.program_id(0)
    t = pl.program_id(1)
    q_pos = q0_ref[t] + lax.broadcasted_iota(jnp.int32, (tq, tk), 0)
    k_pos = k0_ref[t] + lax.broadcasted_iota(jnp.int32, (tq, tk), 1)
    rel = k_pos - q_pos
    n = jnp.abs(rel)
    large = jnp.full((tq, tk), 8, jnp.int32)
    for thr in (12, 16, 23, 32, 46, 64, 91):
        large = large + (n >= thr).astype(jnp.int32)
    bucket = jnp.where(rel > 0, N_BUCKETS // 2, 0) + jnp.where(n < 8, n, large)
    bias = jnp.zeros((tq, tk), F32)
    for b in range(N_BUCKETS):
        bias = jnp.where(bucket == b, tab_ref[b, h], bias)
    visible = (k_pos // CHUNK) <= (q_pos // CHUNK)
    o_ref[...] = jnp.where(visible, bias, NEG)


def _bias_tiles(rel_table, q0, k0, tq, tk):
    n = q0.shape[0]
    return pl.pallas_call(
        functools.partial(_bias_kernel, tq=tq, tk=tk),
        grid_spec=pltpu.PrefetchScalarGridSpec(
            num_scalar_prefetch=2,
            grid=(DIFF_HEADS, n),
            in_specs=[pl.BlockSpec(memory_space=pltpu.SMEM)],
            out_specs=pl.BlockSpec((None, None, tq, tk), lambda h, t, *_: (h, t, 0, 0)),
        ),
        out_shape=jax.ShapeDtypeStruct((DIFF_HEADS, n, tq, tk), F32),
        compiler_params=_params("parallel", "parallel"),
        name="bias_tiles",
    )(q0, k0, rel_table)


def _diff_step(q, k, v, bias, m_sc, l_sc, acc_sc):
    kb = k.astype(BF16)
    vb = v.astype(BF16)
    for c in range(2):
        s = _dot_nt(q[:, c * HEAD_DIM:(c + 1) * HEAD_DIM], kb[:, c * HEAD_DIM:(c + 1) * HEAD_DIM]) * SCALE + bias
        m_prev = m_sc[c]
        m_new = jnp.maximum(m_prev, jnp.max(s, axis=1, keepdims=True))
        alpha = jnp.exp(m_prev - m_new)
        p = jnp.exp(s - m_new)
        l_sc[c] = alpha * l_sc[c] + jnp.sum(p, axis=1, keepdims=True)
        acc_sc[c] = alpha * acc_sc[c] + _dot(p.astype(BF16), vb)
        m_sc[c] = m_new


def _diff_kernel(*refs, has_new, lam_init):
    if has_new:
        lam_ref, g_ref, q_ref, k_ref, v_ref, bias_ref, kn_ref, vn_ref, biasn_ref, o_ref, m_sc, l_sc, acc_sc = refs
    else:
        lam_ref, g_ref, q_ref, k_ref, v_ref, bias_ref, o_ref, m_sc, l_sc, acc_sc = refs
    qi = pl.program_id(2)
    kk = pl.program_id(3)

    @pl.when(kk == 0)
    def _():
        m_sc[...] = jnp.full(m_sc.shape, NEG, F32)
        l_sc[...] = jnp.zeros(l_sc.shape, F32)
        acc_sc[...] = jnp.zeros(acc_sc.shape, F32)

    if has_new:
        @pl.when(kk == 0)
        def _():
            _diff_step(q_ref[...], kn_ref[...], vn_ref[...], biasn_ref[...], m_sc, l_sc, acc_sc)

        @pl.when(kk > 0)
        def _():
            _diff_step(q_ref[...], k_ref[...], v_ref[...], bias_ref[...], m_sc, l_sc, acc_sc)
    else:
        @pl.when(kk <= qi)
        def _():
            _diff_step(q_ref[...], k_ref[...], v_ref[...], bias_ref[...], m_sc, l_sc, acc_sc)

    @pl.when(kk == pl.num_programs(3) - 1)
    def _():
        lp = lam_ref[...]
        lam = (jnp.exp(jnp.sum(lp[0:1] * lp[1:2], axis=1, keepdims=True))
               - jnp.exp(jnp.sum(lp[2:3] * lp[3:4], axis=1, keepdims=True)) + lam_init)
        o = acc_sc[0] / l_sc[0] - lam * (acc_sc[1] / l_sc[1])
        o = o * lax.rsqrt(jnp.mean(o * o, axis=1, keepdims=True) + LN_EPS) * g_ref[...]
        o_ref[...] = (o * (1.0 - lam_init)).astype(o_ref.dtype)


def _diff_attention(q, k, v, bias, lam_vec, subln_g, layer, tq, tk, new=None):
    B, Tq, _ = q.shape
    lam_init = 0.8 - 0.6 * math.exp(-0.3 * layer)
    has_new = new is not None
    if has_new:
        nkc = k.shape[2] // tk
        nk = nkc + 1
        kv_spec = pl.BlockSpec((None, None, tk, DIFF_VDIM),
                               lambda b, h, qi, kk: (layer, b, jnp.clip(nkc - kk, 0, nkc - 1), h))
        bias_spec = pl.BlockSpec((None, None, tq, tk), lambda b, h, qi, kk: (h, jnp.clip(nkc - kk, 0, nkc - 1), 0, 0))
        tn = new[0].shape[1]
        new_specs = [pl.BlockSpec((None, tn, DIFF_VDIM), lambda b, h, qi, kk: (b, 0, h)),
                     pl.BlockSpec((None, tn, DIFF_VDIM), lambda b, h, qi, kk: (b, 0, h)),
                     pl.BlockSpec((None, None, tq, tn), lambda b, h, qi, kk: (h, 0, 0, 0))]
        new_args = list(new)
    else:
        nk = k.shape[1] // tk
        kv_spec = pl.BlockSpec((None, tk, DIFF_VDIM), lambda b, h, qi, kk: (b, jnp.maximum(qi - kk, 0), h))
        bias_spec = pl.BlockSpec((None, None, tq, tk), lambda b, h, qi, kk: (h, jnp.minimum(kk, qi), 0, 0))
        new_specs, new_args = [], []
    return pl.pallas_call(
        functools.partial(_diff_kernel, has_new=has_new, lam_init=lam_init),
        grid=(B, DIFF_HEADS, Tq // tq, nk),
        in_specs=[
            pl.BlockSpec((4, HEAD_DIM), lambda b, h, qi, kk: (0, 0)),
            pl.BlockSpec((1, DIFF_VDIM), lambda b, h, qi, kk: (0, 0)),
            pl.BlockSpec((None, tq, DIFF_VDIM), lambda b, h, qi, kk: (b, qi, h)),
            kv_spec, kv_spec, bias_spec, *new_specs,
        ],
        out_specs=pl.BlockSpec((None, tq, DIFF_VDIM), lambda b, h, qi, kk: (b, qi, h)),
        out_shape=jax.ShapeDtypeStruct((B, Tq, SELF_WIDTH), BF16),
        scratch_shapes=[pltpu.VMEM((2, tq, 1), F32), pltpu.VMEM((2, tq, 1), F32),
                        pltpu.VMEM((2, tq, DIFF_VDIM), F32)],
        compiler_params=_params("parallel", "parallel", "parallel", "arbitrary"),
        name="diff_attention",
    )(lam_vec, subln_g, q, k, v, bias, *new_args)


def _sb_step(q, k_ref, v_ref, q_pos0, k_pos0, acc_sc, r_sc, cw, masked):
    tq = q.shape[0]
    tk = k_ref.shape[0]
    cw = min(cw, tk)
    later = (lax.broadcasted_iota(jnp.int32, (cw, cw), 0) > lax.broadcasted_iota(jnp.int32, (cw, cw), 1))
    later = jnp.where(later, 1.0, 0.0).astype(BF16)
    for c0 in range(tk - cw, -1, -cw):
        zc = _dot_nt(q, k_ref[c0:c0 + cw, :].astype(BF16)) * SCALE
        l1p = jnp.log(1.0 + jnp.exp(-jnp.abs(zc)))
        log_keep = -(jnp.maximum(zc, 0.0) + l1p)
        log_beta = jnp.minimum(zc, 0.0) - l1p
        if masked:
            q_pos = q_pos0 + lax.broadcasted_iota(jnp.int32, (tq, cw), 0)
            before = (k_pos0 + c0 + lax.broadcasted_iota(jnp.int32, (tq, cw), 1)) < q_pos
            log_keep = jnp.where(before, log_keep, 0.0)
        hi = log_keep.astype(BF16)
        lo = (log_keep - hi.astype(F32)).astype(BF16)
        between = _dot(hi, later) + _dot(lo, later) + r_sc[...]
        a = jnp.exp(log_beta + between)
        if masked:
            a = jnp.where(before, a, 0.0)
        acc_sc[...] += _dot(a.astype(BF16), v_ref[c0:c0 + cw, :].astype(BF16))
        r_sc[...] += jnp.sum(log_keep, axis=1, keepdims=True)


def _sb_kernel(*refs, has_new, tq, tk, q_off, nkc, cw):
    if has_new:
        q_ref, k_ref, v_ref, kn_ref, vn_ref, o_ref, acc_sc, r_sc = refs
    else:
        q_ref, k_ref, v_ref, o_ref, acc_sc, r_sc = refs
    qi = pl.program_id(2)
    kk = pl.program_id(3)

    @pl.when(kk == 0)
    def _():
        acc_sc[...] = jnp.zeros(acc_sc.shape, F32)
        r_sc[...] = jnp.zeros(r_sc.shape, F32)

    if has_new:
        @pl.when(kk == 0)
        def _():
            _sb_step(q_ref[...], kn_ref, vn_ref, q_off, q_off, acc_sc, r_sc, cw, True)

        @pl.when(kk > 0)
        def _():
            _sb_step(q_ref[...], k_ref, v_ref, q_off, (nkc - kk) * tk, acc_sc, r_sc, cw, False)
    else:
        @pl.when(kk == 0)
        def _():
            _sb_step(q_ref[...], k_ref, v_ref, qi * tq, qi * tk, acc_sc, r_sc, cw, True)

        @pl.when((kk > 0) & (kk <= qi))
        def _():
            _sb_step(q_ref[...], k_ref, v_ref, qi * tq, (qi - kk) * tk, acc_sc, r_sc, cw, False)

    @pl.when(kk == pl.num_programs(3) - 1)
    def _():
        o_ref[...] = acc_sc[...].astype(o_ref.dtype)


def _sb_attention(q, k, v, layer, tq, tk, new=None):
    B, Tq, _ = q.shape
    has_new = new is not None
    if has_new:
        nkc = k.shape[2] // tk
        nk = nkc + 1
        q_off = k.shape[2]
        kv_spec = pl.BlockSpec((None, None, tk, HEAD_DIM),
                               lambda b, h, qi, kk: (layer, b, jnp.clip(nkc - kk, 0, nkc - 1), h))
        tn = new[0].shape[1]
        new_specs = [pl.BlockSpec((None, tn, HEAD_DIM), lambda b, h, qi, kk: (b, 0, h))] * 2
        new_args = list(new)
    else:
        assert tq == tk
        nkc = k.shape[1] // tk
        nk = nkc
        q_off = 0
        kv_spec = pl.BlockSpec((None, tk, HEAD_DIM), lambda b, h, qi, kk: (b, jnp.maximum(qi - kk, 0), h))
        new_specs, new_args = [], []
    return pl.pallas_call(
        functools.partial(_sb_kernel, has_new=has_new, tq=tq, tk=tk, q_off=q_off, nkc=nkc, cw=256),
        grid=(B, SB_HEADS, Tq // tq, nk),
        in_specs=[pl.BlockSpec((None, tq, HEAD_DIM), lambda b, h, qi, kk: (b, qi, h)), kv_spec, kv_spec, *new_specs],
        out_specs=pl.BlockSpec((None, tq, HEAD_DIM), lambda b, h, qi, kk: (b, qi, h)),
        out_shape=jax.ShapeDtypeStruct((B, Tq, SELF_WIDTH), BF16),
        scratch_shapes=[pltpu.VMEM((tq, HEAD_DIM), F32), pltpu.VMEM((tq, 1), F32)],
        compiler_params=_params("parallel", "parallel", "parallel", "arbitrary"),
        name="sb_attention",
    )(q, k, v, *new_args)


def _mem_kernel(q_ref, k_ref, v_ref, o_ref):
    q = q_ref[...]
    kb = k_ref[...].astype(BF16)
    vb = v_ref[...].astype(BF16)
    for h in range(MEM_HEADS):
        sl = slice(h * HEAD_DIM, (h + 1) * HEAD_DIM)
        s = _dot_nt(q[:, sl], kb[:, sl]) * SCALE
        p = jnp.exp(s - jnp.max(s, axis=1, keepdims=True))
        o = _dot(p.astype(BF16), vb[:, sl]) / jnp.sum(p, axis=1, keepdims=True)
        o_ref[:, sl] = o.astype(o_ref.dtype)


def _mem_attention(mq, mk, mv, tq, layer=None):
    B, Tq, _ = mq.shape
    n_mem = mk.shape[-2]
    if layer is None:
        kv_spec = pl.BlockSpec((None, n_mem, MEM_WIDTH), lambda b, qi: (b, 0, 0))
    else:
        kv_spec = pl.BlockSpec((None, None, n_mem, MEM_WIDTH), lambda b, qi: (layer, b, 0, 0))
    return pl.pallas_call(
        _mem_kernel,
        grid=(B, Tq // tq),
        in_specs=[pl.BlockSpec((None, tq, MEM_WIDTH), lambda b, qi: (b, qi, 0)), kv_spec, kv_spec],
        out_specs=pl.BlockSpec((None, tq, MEM_WIDTH), lambda b, qi: (b, qi, 0)),
        out_shape=jax.ShapeDtypeStruct((B, Tq, MEM_WIDTH), BF16),
        compiler_params=_params("parallel", "parallel"),
        name="mem_attention",
    )(mq, mk, mv)


def _top_values(x, n):
    out = []
    for _ in range(n):
        m = jnp.max(x, axis=0, keepdims=True)
        out.append(m)
        x = jnp.where(x == m, -jnp.inf, x)
    return out


def _peer_select_kernel(q_ref, keys_ref, thr_ref, e1z_ref, s2_ref, e2_ref, cand_sc):
    n_top = PEER_TOPK + 1
    pairs = [(p, q) for p in range(n_top) for q in range(n_top) if (p + 1) * (q + 1) <= n_top]
    for h in range(PEER_HEADS):
        s = []
        for c in range(2):
            col = (2 * h + c) * HEAD_DIM
            s.append(_dot_nt(keys_ref[h, c], q_ref[:, col:col + HEAD_DIM]))
        a = _top_values(s[0], n_top)
        b = _top_values(s[1], n_top)
        cand_sc[...] = jnp.full(cand_sc.shape, -jnp.inf, F32)
        for r, (p, q) in enumerate(pairs):
            cand_sc[r:r + 1, :] = a[p] + b[q]
        c = _top_values(cand_sc[...], n_top)
        z = jnp.ones_like(c[0])
        for kth in range(1, PEER_TOPK):
            z = z + jnp.exp(c[kth] - c[0])
        tau = 0.5 * (c[PEER_TOPK - 1] + c[PEER_TOPK])
        thr_ref[h] = tau - s[0]
        e1z_ref[h] = jnp.exp(s[0] - a[0]) / z
        e2 = jnp.exp(s[1] - b[0])
        for lg in range(s2_ref.shape[1]):
            s2_ref[h, lg] = s[1][:, lg * LANES:(lg + 1) * LANES]
            e2_ref[h, lg] = e2[:, lg * LANES:(lg + 1) * LANES]


def _peer_select(qp, keys, layer, tt):
    T = qp.shape[0]
    tt = _tile(T, tt)
    n_pairs = sum(1 for p in range(1, PEER_TOPK + 2) for q in range(1, PEER_TOPK + 2) if p * q <= PEER_TOPK + 1)
    out = jax.ShapeDtypeStruct((PEER_HEADS, N_KEYS, T), F32)
    spec = pl.BlockSpec((PEER_HEADS, N_KEYS, tt), lambda i: (0, 0, i))
    out_lg = jax.ShapeDtypeStruct((PEER_HEADS, T // LANES, N_KEYS, LANES), F32)
    spec_lg = pl.BlockSpec((PEER_HEADS, tt // LANES, N_KEYS, LANES), lambda i: (0, i, 0, 0))
    return pl.pallas_call(
        _peer_select_kernel,
        grid=(T // tt,),
        in_specs=[pl.BlockSpec((tt, PEER_HEADS * 2 * HEAD_DIM), lambda i: (i, 0)),
                  pl.BlockSpec((None, PEER_HEADS, 2, N_KEYS, HEAD_DIM), lambda i: (layer, 0, 0, 0, 0))],
        out_specs=[spec, spec, spec_lg, spec_lg],
        out_shape=[out, out, out_lg, out_lg],
        scratch_shapes=[pltpu.VMEM((-(-n_pairs // 8) * 8, tt), F32)],
        compiler_params=_params("parallel"),
        name="peer_select",
    )(qp, keys)


def _peer_gate(thr_ref, e1z_ref, s2_ref, e2_ref, act_ref, hid_ref, rows, lg, sb):
    ls = slice(lg * LANES, (lg + 1) * LANES)
    js = slice(sb * GATE_SLAB, (sb + 1) * GATE_SLAB)
    gates = [None] * rows
    for h in range(PEER_HEADS):
        s2 = s2_ref[h, lg, js, :]
        e2 = e2_ref[h, lg, js, :]
        for ii in range(rows):
            thr = jnp.broadcast_to(thr_ref[h, ii:ii + 1, ls], (GATE_SLAB, LANES))
            e1z = jnp.broadcast_to(e1z_ref[h, ii:ii + 1, ls], (GATE_SLAB, LANES))
            g = jnp.where(s2 >= thr, e2 * e1z, 0.0)
            gates[ii] = g if gates[ii] is None else gates[ii] + g
    for ii in range(rows):
        es = slice(ii * N_KEYS + sb * GATE_SLAB, ii * N_KEYS + (sb + 1) * GATE_SLAB)
        a = act_ref[es, ls]
        hid_ref[es, ls] = (0.5 * a * (1.0 + lax.erf(a * (2.0 ** -0.5))) * gates[ii]).astype(BF16)


def _peer_mix_kernel(xt_ref, u_ref, vt_ref, thr_ref, e1z_ref, s2_ref, e2_ref, h_ref, g_ref, b_ref,
                     y_ref, acc_sc, act0, act1, hid0, hid1, *, rows, ne, alpha):
    s = pl.program_id(1)
    tt = xt_ref.shape[1]

    @pl.when(s == 0)
    def _():
        acc_sc[...] = jnp.zeros(acc_sc.shape, F32)
        for r in (act0, act1, hid0, hid1):
            r[...] = jnp.zeros(r.shape, r.dtype)

    def stages(act_w, act_r, hid_w, hid_r):
        mt = MXU_TILE
        mp = PEER_MM_ROWS
        pieces = []
        kh = xt_ref.shape[0] // 2
        for n in range(tt // mt):
            for m in range(act_w.shape[0] // mp):
                for k in range(2):
                    def mm1(m=m, n=n, k=k):
                        d = _dot(u_ref[m * mp:(m + 1) * mp, k * kh:(k + 1) * kh],
                                 xt_ref[k * kh:(k + 1) * kh, n * mt:(n + 1) * mt])
                        if k == 0:
                            act_w[m * mp:(m + 1) * mp, n * mt:(n + 1) * mt] = d
                        else:
                            act_w[m * mp:(m + 1) * mp, n * mt:(n + 1) * mt] += d
                    pieces.append((mm1, kh // mt))
        for m in range(acc_sc.shape[0] // mp):
            def mm2(m=m):
                acc_sc[m * mp:(m + 1) * mp, :] += _dot(vt_ref[m * mp:(m + 1) * mp, :], hid_r[...])
            pieces.append((mm2, (hid_r.shape[0] // mt) * (tt // mt)))
        groups = [(lg, sb) for lg in range(tt // LANES) for sb in range(N_KEYS // GATE_SLAB)]
        total = sum(c for _, c in pieces)
        issued = 0
        for gi, (lg, sb) in enumerate(groups):
            while pieces and issued * len(groups) <= gi * total:
                fn, cost = pieces.pop(0)
                fn()
                issued += cost
            _peer_gate(thr_ref, e1z_ref, s2_ref, e2_ref, act_r, hid_w, rows, lg, sb)
        for fn, _ in pieces:
            fn()

    @pl.when(s % 2 == 0)
    def _():
        stages(act0, act1, hid1, hid0)

    @pl.when(s % 2 == 1)
    def _():
        stages(act1, act0, hid0, hid1)

    @pl.when(s == ne + 1)
    def _():
        y_ref[...] = _layer_norm(alpha * h_ref[...] + acc_sc[...].T, g_ref[...], b_ref[...])


def _peer_mix(xt, u, vt, sel, layer, h, ln_g, ln_b, alpha, tt, rows):
    D, T = xt.shape
    tt = _tile(T, tt)
    n_exp = u.shape[1]
    eb = rows * N_KEYS
    assert n_exp % eb == 0
    ne = n_exp // eb
    row_spec = pl.BlockSpec((PEER_HEADS, None, rows, tt), lambda t, s: (0, jnp.clip(s - 1, 0, ne - 1), 0, t))
    full_spec = pl.BlockSpec((PEER_HEADS, tt // LANES, N_KEYS, LANES), lambda t, s: (0, t, 0, 0))
    thr, e1z, s2, e2 = sel
    thr = thr.reshape(PEER_HEADS, ne, rows, T)
    e1z = e1z.reshape(PEER_HEADS, ne, rows, T)
    return pl.pallas_call(
        functools.partial(_peer_mix_kernel, rows=rows, ne=ne, alpha=alpha),
        grid=(T // tt, ne + 2),
        in_specs=[
            pl.BlockSpec((D, tt), lambda t, s: (0, t)),
            pl.BlockSpec((None, eb, D), lambda t, s: (layer, jnp.minimum(s, ne - 1), 0)),
            pl.BlockSpec((None, D, eb), lambda t, s: (layer, 0, jnp.clip(s - 2, 0, ne - 1))),
            row_spec, row_spec, full_spec, full_spec,
            pl.BlockSpec((tt, D), lambda t, s: (t, 0), pipeline_mode=pl.Buffered(1)),
            pl.BlockSpec((None, 1, D), lambda t, s: (layer, 0, 0)),
            pl.BlockSpec((None, 1, D), lambda t, s: (layer, 0, 0)),
        ],
        out_specs=pl.BlockSpec((tt, D), lambda t, s: (t, 0)),
        out_shape=jax.ShapeDtypeStruct((T, D), F32),
        scratch_shapes=[pltpu.VMEM((D, tt), F32), pltpu.VMEM((eb, tt), F32), pltpu.VMEM((eb, tt), F32),
                        pltpu.VMEM((eb, tt), BF16), pltpu.VMEM((eb, tt), BF16)],
        compiler_params=_params("parallel", "arbitrary"),
        name="peer_mix",
    )(xt, u, vt, thr, e1z, s2, e2, h, ln_g, ln_b)


def kernel(x_prompt, x_sample, cache_self_k, cache_self_v, cache_mem_k, cache_mem_v, mem_prompt, w_in, w_o, w_mem_k, w_mem_v, rel_bias_table, diff_lambda, diff_subln_g, ln_g, ln_b, peer_w_q, peer_sub_keys, peer_u, peer_v):
    depth = w_in.shape[0]
    bp, n_prompt, d_model = x_prompt.shape
    bs, n_new, _ = x_sample.shape
    n_past = cache_self_k.shape[2]
    n_mem = mem_prompt.shape[1]
    alpha = (2 * depth) ** 0.25
    S = SELF_WIDTH

    w_in_b = w_in.astype(BF16)
    w_o_b = w_o.astype(BF16)
    w_mk_b = w_mem_k.astype(BF16)
    w_mv_b = w_mem_v.astype(BF16)
    w_pq_b = peer_w_q.astype(BF16)
    keys_b = peer_sub_keys.astype(BF16)
    u_b = peer_u.astype(BF16)
    vt_b = jnp.swapaxes(peer_v, 1, 2).astype(BF16)
    ln_g4 = ln_g.reshape(depth, 2, 1, d_model)
    ln_b4 = ln_b.reshape(depth, 2, 1, d_model)
    cmk = cache_mem_k.reshape(depth, bs, n_mem, MEM_WIDTH)
    cmv = cache_mem_v.reshape(depth, bs, n_mem, MEM_WIDTH)
    mem2d = mem_prompt.reshape(bp * n_mem, d_model)

    tq_p = _tile(n_prompt, 512)
    tk_s = _tile(n_past, 512)
    nq_p = n_prompt // tq_p
    nk_s = n_past // tk_s
    i32 = jnp.int32
    bias_p = _bias_tiles(rel_bias_table, jnp.arange(nq_p, dtype=i32) * tq_p, jnp.zeros((nq_p,), i32), tq_p, tq_p)
    bias_sc = _bias_tiles(rel_bias_table, jnp.full((nk_s,), n_past, i32), jnp.arange(nk_s, dtype=i32) * tk_s, n_new, tk_s)
    bias_sn = _bias_tiles(rel_bias_table, jnp.full((1,), n_past, i32), jnp.full((1,), n_past, i32), n_new, n_new)

    hp = x_prompt.reshape(bp * n_prompt, d_model)
    hs = x_sample.reshape(bs * n_new, d_model)
    segs = ((0, S), (S, S), (2 * S, S), (3 * S, MEM_WIDTH))
    seg_dt = (BF16, F32, F32, BF16)
    new_k_p, new_v_p, new_mk_p, new_mv_p, new_k_s, new_v_s = [], [], [], [], [], []
    for i in range(depth):
        kind = i % 2
        j = i // 2
        qp, kp, vp, mqp = _linear(hp, w_in_b, i, segs, seg_dt, 256)
        qs, ks, vs, mqs = _linear(hs, w_in_b, i, segs, seg_dt, 256)
        (mkp,) = _linear(mem2d, w_mk_b, i, ((0, MEM_WIDTH),), (F32,), 512)
        (mvp,) = _linear(mem2d, w_mv_b, i, ((0, MEM_WIDTH),), (F32,), 512)
        qp3, kp3, vp3 = (a.reshape(bp, n_prompt, S) for a in (qp, kp, vp))
        qs3, ks3, vs3 = (a.reshape(bs, n_new, S) for a in (qs, ks, vs))

        if kind == 0:
            g = diff_subln_g[j].reshape(1, DIFF_VDIM)
            op = _diff_attention(qp3, kp3, vp3, bias_p, diff_lambda[j], g, i, tq_p, tq_p)
            os_ = _diff_attention(qs3, cache_self_k, cache_self_v, bias_sc, diff_lambda[j], g, i, n_new, tk_s,
                                  new=(ks3, vs3, bias_sn))
        else:
            op = _sb_attention(qp3, kp3, vp3, i, tq_p, tq_p)
            os_ = _sb_attention(qs3, cache_self_k, cache_self_v, i, n_new, tk_s, new=(ks3, vs3))

        mop = _mem_attention(mqp.reshape(bp, n_prompt, MEM_WIDTH), mkp.reshape(bp, n_mem, MEM_WIDTH),
                             mvp.reshape(bp, n_mem, MEM_WIDTH), tq_p)
        mos = _mem_attention(mqs.reshape(bs, n_new, MEM_WIDTH), cmk, cmv, n_new, layer=i)

        outs = []
        for h_res, o_self, o_mem in ((hp, op, mop), (hs, os_, mos)):
            T = h_res.shape[0]
            y, yt = _out_proj(o_self.reshape(T, S), o_mem.reshape(T, MEM_WIDTH), w_o_b, i, h_res,
                              ln_g4[:, 0], ln_b4[:, 0], alpha, 256)
            (pq,) = _linear(y, w_pq_b, i, ((0, PEER_HEADS * 2 * HEAD_DIM),), (BF16,), 256)
            sel = _peer_select(pq, keys_b, i, 512)
            outs.append(_peer_mix(yt, u_b, vt_b, sel, i, y, ln_g4[:, 1], ln_b4[:, 1], alpha, 512, 4))
        hp, hs = outs

        new_k_p.append(kp3)
        new_v_p.append(vp3)
        new_mk_p.append(mkp.reshape(bp, n_mem, MEM_HEADS, HEAD_DIM))
        new_mv_p.append(mvp.reshape(bp, n_mem, MEM_HEADS, HEAD_DIM))
        new_k_s.append(ks3)
        new_v_s.append(vs3)

    return (hp.reshape(bp, n_prompt, d_model), hs.reshape(bs, n_new, d_model),
            jnp.stack(new_k_p), jnp.stack(new_v_p), jnp.stack(new_mk_p), jnp.stack(new_mv_p),
            jnp.stack(new_k_s), jnp.stack(new_v_s))
```

```python
import functools
import math

import jax
import jax.numpy as jnp
from jax import lax
from jax.experimental import pallas as pl
from jax.experimental.pallas import tpu as pltpu

F32 = jnp.float32
BF16 = jnp.bfloat16

HEAD_DIM = 128
SELF_WIDTH = 1536
DIFF_HEADS = 6
DIFF_VDIM = 2 * HEAD_DIM
SB_HEADS = 12
MEM_HEADS = 4
MEM_WIDTH = MEM_HEADS * HEAD_DIM
CHUNK = 64
N_BUCKETS = 32
PEER_HEADS = 8
N_KEYS = 128
PEER_TOPK = 16
LN_EPS = 1e-5
SCALE = HEAD_DIM ** -0.5
NEG = -1e30

V7X_VMEM_LIMIT = 56 * 1024 * 1024
LANES = 128
MXU_TILE = 256
SB_PAIR = 2
PEER_ROWS = 4
GATE_ACC_VREGS = 16


def _params(*sem):
    return pltpu.CompilerParams(dimension_semantics=sem, vmem_limit_bytes=V7X_VMEM_LIMIT)


def _tile(n, t):
    t = min(n, t)
    assert n % t == 0, (n, t)
    return t


def _dot(a, b):
    return jnp.dot(a, b, preferred_element_type=F32)


def _dot_nt(a, b):
    return lax.dot_general(a, b, (((1,), (1,)), ((), ())), preferred_element_type=F32)


def _layer_norm(r, g, b):
    mu = jnp.mean(r, axis=-1, keepdims=True)
    d = r - mu
    var = jnp.mean(d * d, axis=-1, keepdims=True)
    return d * lax.rsqrt(var + LN_EPS) * g + b


def _linear_kernel(x_ref, w_ref, *o_refs, segs, nc):
    x = x_ref[...].astype(BF16)
    for o_ref, (start, width) in zip(o_refs, segs):
        for c in range(0, width, nc):
            o_ref[:, c:c + nc] = _dot(x, w_ref[:, start + c:start + c + nc]).astype(o_ref.dtype)


def _linear(x, w, layer, segs, dtypes, tm):
    M, K = x.shape
    N = w.shape[2]
    tm = _tile(M, tm)
    nc = 512
    assert all(wd % nc == 0 for _, wd in segs)
    return pl.pallas_call(
        functools.partial(_linear_kernel, segs=segs, nc=nc),
        grid=(M // tm,),
        in_specs=[
            pl.BlockSpec((tm, K), lambda i: (i, 0)),
            pl.BlockSpec((None, K, N), lambda i: (layer, 0, 0), pipeline_mode=pl.Buffered(1)),
        ],
        out_specs=[pl.BlockSpec((tm, wd), lambda i: (i, 0)) for _, wd in segs],
        out_shape=[jax.ShapeDtypeStruct((M, wd), dt) for (_, wd), dt in zip(segs, dtypes)],
        compiler_params=_params("parallel"),
        name="linear",
    )(x, w)


def _out_proj_kernel(o_ref, mo_ref, w_ref, h_ref, g_ref, b_ref, y_ref, yt_ref, *, alpha):
    acc = _dot(o_ref[...], w_ref[:SELF_WIDTH, :]) + _dot(mo_ref[...], w_ref[SELF_WIDTH:, :])
    y = _layer_norm(alpha * h_ref[...] + acc, g_ref[...], b_ref[...])
    y_ref[...] = y
    yt_ref[...] = y.T.astype(BF16)


def _out_proj(o, mo, w_o, layer, h, ln_g, ln_b, alpha, tm):
    T, D = h.shape
    tm = _tile(T, tm)
    return pl.pallas_call(
        functools.partial(_out_proj_kernel, alpha=alpha),
        grid=(T // tm,),
        in_specs=[
            pl.BlockSpec((tm, SELF_WIDTH), lambda i: (i, 0)),
            pl.BlockSpec((tm, MEM_WIDTH), lambda i: (i, 0)),
            pl.BlockSpec((None, SELF_WIDTH + MEM_WIDTH, D), lambda i: (layer, 0, 0), pipeline_mode=pl.Buffered(1)),
            pl.BlockSpec((tm, D), lambda i: (i, 0)),
            pl.BlockSpec((None, 1, D), lambda i: (layer, 0, 0)),
            pl.BlockSpec((None, 1, D), lambda i: (layer, 0, 0)),
        ],
        out_specs=[pl.BlockSpec((tm, D), lambda i: (i, 0)), pl.BlockSpec((D, tm), lambda i: (0, i))],
        out_shape=[jax.ShapeDtypeStruct((T, D), F32), jax.ShapeDtypeStruct((D, T), BF16)],
        compiler_params=_params("parallel"),
        name="out_proj_ln",
    )(o, mo, w_o, h, ln_g, ln_b)


def _bias_kernel(q0_ref, k0_ref, tab_ref, o_ref, *, tq, tk):
    h = pl.program_id(0)
    t = pl.program_id(1)
    q_pos = q0_ref[t] + lax.broadcasted_iota(jnp.int32, (tq, tk), 0)
    k_pos = k0_ref[t] + lax.broadcasted_iota(jnp.int32, (tq, tk), 1)
    rel = k_pos - q_pos
    n = jnp.abs(rel)
    large = jnp.full((tq, tk), 8, jnp.int32)
    for thr in (12, 16, 23, 32, 46, 64, 91):
        large = large + (n >= thr).astype(jnp.int32)
    bucket = jnp.where(rel > 0, N_BUCKETS // 2, 0) + jnp.where(n < 8, n, large)
    bias = jnp.zeros((tq, tk), F32)
    for b in range(N_BUCKETS):
        bias = jnp.where(bucket == b, tab_ref[b, h], bias)
    visible = (k_pos // CHUNK) <= (q_pos // CHUNK)
    o_ref[...] = jnp.where(visible, bias, NEG)


def _bias_tiles(rel_table, q0, k0, tq, tk):
    n = q0.shape[0]
    return pl.pallas_call(
        functools.partial(_bias_kernel, tq=tq, tk=tk),
        grid_spec=pltpu.PrefetchScalarGridSpec(
            num_scalar_prefetch=2,
            grid=(DIFF_HEADS, n),
            in_specs=[pl.BlockSpec(memory_space=pltpu.SMEM)],
            out_specs=pl.BlockSpec((None, None, tq, tk), lambda h, t, *_: (h, t, 0, 0)),
        ),
        out_shape=jax.ShapeDtypeStruct((DIFF_HEADS, n, tq, tk), F32),
        compiler_params=_params("parallel", "parallel"),
        name="bias_tiles",
    )(q0, k0, rel_table)


def _diff_step(q, k, v, bias, m_sc, l_sc, acc_sc):
    kb = k.astype(BF16)
    vb = v.astype(BF16)
    for c in range(2):
        s = _dot_nt(q[:, c * HEAD_DIM:(c + 1) * HEAD_DIM], kb[:, c * HEAD_DIM:(c + 1) * HEAD_DIM]) * SCALE + bias
        m_prev = m_sc[c]
        m_new = jnp.maximum(m_prev, jnp.max(s, axis=1, keepdims=True))
        alpha = jnp.exp(m_prev - m_new)
        p = jnp.exp(s - m_new)
        l_sc[c] = alpha * l_sc[c] + jnp.sum(p, axis=1, keepdims=True)
        acc_sc[c] = alpha * acc_sc[c] + _dot(p.astype(BF16), vb)
        m_sc[c] = m_new


def _diff_kernel(*refs, has_new, lam_init):
    if has_new:
        lam_ref, g_ref, q_ref, k_ref, v_ref, bias_ref, kn_ref, vn_ref, biasn_ref, o_ref, m_sc, l_sc, acc_sc = refs
    else:
        lam_ref, g_ref, q_ref, k_ref, v_ref, bias_ref, o_ref, m_sc, l_sc, acc_sc = refs
    qi = pl.program_id(2)
    kk = pl.program_id(3)

    @pl.when(kk == 0)
    def _():
        m_sc[...] = jnp.full(m_sc.shape, NEG, F32)
        l_sc[...] = jnp.zeros(l_sc.shape, F32)
        acc_sc[...] = jnp.zeros(acc_sc.shape, F32)

    if has_new:
        @pl.when(kk == 0)
        def _():
            _diff_step(q_ref[...], kn_ref[...], vn_ref[...], biasn_ref[...], m_sc, l_sc, acc_sc)

        @pl.when(kk > 0)
        def _():
            _diff_step(q_ref[...], k_ref[...], v_ref[...], bias_ref[...], m_sc, l_sc, acc_sc)
    else:
        @pl.when(kk <= qi)
        def _():
            _diff_step(q_ref[...], k_ref[...], v_ref[...], bias_ref[...], m_sc, l_sc, acc_sc)

    @pl.when(kk == pl.num_programs(3) - 1)
    def _():
        lp = lam_ref[...]
        lam = (jnp.exp(jnp.sum(lp[0:1] * lp[1:2], axis=1, keepdims=True))
               - jnp.exp(jnp.sum(lp[2:3] * lp[3:4], axis=1, keepdims=True)) + lam_init)
        o = acc_sc[0] / l_sc[0] - lam * (acc_sc[1] / l_sc[1])
        o = o * lax.rsqrt(jnp.mean(o * o, axis=1, keepdims=True) + LN_EPS) * g_ref[...]
        o_ref[...] = (o * (1.0 - lam_init)).astype(o_ref.dtype)


def _diff_attention(q, k, v, bias, lam_vec, subln_g, layer, tq, tk, new=None):
    B, Tq, _ = q.shape
    lam_init = 0.8 - 0.6 * math.exp(-0.3 * layer)
    has_new = new is not None
    if has_new:
        nkc = k.shape[2] // tk
        nk = nkc + 1
        kv_spec = pl.BlockSpec((None, None, tk, DIFF_VDIM),
                               lambda b, h, qi, kk: (layer, b, jnp.clip(nkc - kk, 0, nkc - 1), h))
        bias_spec = pl.BlockSpec((None, None, tq, tk), lambda b, h, qi, kk: (h, jnp.clip(nkc - kk, 0, nkc - 1), 0, 0))
        tn = new[0].shape[1]
        new_specs = [pl.BlockSpec((None, tn, DIFF_VDIM), lambda b, h, qi, kk: (b, 0, h)),
                     pl.BlockSpec((None, tn, DIFF_VDIM), lambda b, h, qi, kk: (b, 0, h)),
                     pl.BlockSpec((None, None, tq, tn), lambda b, h, qi, kk: (h, 0, 0, 0))]
        new_args = list(new)
    else:
        nk = k.shape[1] // tk
        kv_spec = pl.BlockSpec((None, tk, DIFF_VDIM), lambda b, h, qi, kk: (b, jnp.maximum(qi - kk, 0), h))
        bias_spec = pl.BlockSpec((None, None, tq, tk), lambda b, h, qi, kk: (h, jnp.minimum(kk, qi), 0, 0))
        new_specs, new_args = [], []
    return pl.pallas_call(
        functools.partial(_diff_kernel, has_new=has_new, lam_init=lam_init),
        grid=(B, DIFF_HEADS, Tq // tq, nk),
        in_specs=[
            pl.BlockSpec((4, HEAD_DIM), lambda b, h, qi, kk: (0, 0)),
            pl.BlockSpec((1, DIFF_VDIM), lambda b, h, qi, kk: (0, 0)),
            pl.BlockSpec((None, tq, DIFF_VDIM), lambda b, h, qi, kk: (b, qi, h)),
            kv_spec, kv_spec, bias_spec, *new_specs,
        ],
        out_specs=pl.BlockSpec((None, tq, DIFF_VDIM), lambda b, h, qi, kk: (b, qi, h)),
        out_shape=jax.ShapeDtypeStruct((B, Tq, SELF_WIDTH), BF16),
        scratch_shapes=[pltpu.VMEM((2, tq, 1), F32), pltpu.VMEM((2, tq, 1), F32),
                        pltpu.VMEM((2, tq, DIFF_VDIM), F32)],
        compiler_params=_params("parallel", "parallel", "parallel", "arbitrary"),
        name="diff_attention",
    )(lam_vec, subln_g, q, k, v, bias, *new_args)


def _sb_step(q_ref, k_ref, v_ref, q_pos0, k_pos0, acc_sc, r_sc, cw, masked):
    tq = q_ref.shape[0]
    tk = k_ref.shape[0]
    cw = min(cw, tk)
    later = (lax.broadcasted_iota(jnp.int32, (cw, cw), 0) > lax.broadcasted_iota(jnp.int32, (cw, cw), 1))
    later = jnp.where(later, 1.0, 0.0).astype(BF16)
    units = [(c0, hh) for c0 in range(tk - cw, -1, -cw) for hh in range(SB_PAIR)]
    live = [dict() for _ in units]

    def scores(u):
        c0, hh = units[u]
        hs = slice(hh * HEAD_DIM, (hh + 1) * HEAD_DIM)
        live[u]["z"] = _dot_nt(q_ref[:, hs], k_ref[c0:c0 + cw, hs].astype(BF16)) * SCALE

    def logs(u):
        c0, _ = units[u]
        z = live[u].pop("z")
        log_keep = -(jnp.maximum(z, 0.0) + jnp.log(1.0 + jnp.exp(-jnp.abs(z))))
        live[u]["log_beta"] = log_keep + z
        if masked:
            q_pos = q_pos0 + lax.broadcasted_iota(jnp.int32, (tq, cw), 0)
            before = (k_pos0 + c0 + lax.broadcasted_iota(jnp.int32, (tq, cw), 1)) < q_pos
            log_keep = jnp.where(before, log_keep, 0.0)
            live[u]["before"] = before
        live[u]["log_keep"] = log_keep

    def cumsum(u):
        live[u]["cum"] = _dot(live[u]["log_keep"].astype(BF16), later)

    def weights(u):
        _, hh = units[u]
        a = jnp.exp(live[u].pop("log_beta") + (live[u].pop("cum") + r_sc[hh]))
        if masked:
            a = jnp.where(live[u].pop("before"), a, 0.0)
        live[u]["a"] = a.astype(BF16)
        r_sc[hh] += jnp.sum(live[u].pop("log_keep"), axis=1, keepdims=True)

    def values(u):
        c0, hh = units[u]
        hs = slice(hh * HEAD_DIM, (hh + 1) * HEAD_DIM)
        acc_sc[hh] += _dot(live[u].pop("a"), v_ref[c0:c0 + cw, hs].astype(BF16))

    stages = (scores, logs, cumsum, weights, values)
    for t in range(len(units) + len(stages) - 1):
        for s in reversed(range(len(stages))):
            if 0 <= t - s < len(units):
                stages[s](t - s)


def _sb_kernel(*refs, has_new, tq, tk, q_off, nkc, cw):
    if has_new:
        q_ref, k_ref, v_ref, kn_ref, vn_ref, o_ref, acc_sc, r_sc = refs
    else:
        q_ref, k_ref, v_ref, o_ref, acc_sc, r_sc = refs
    qi = pl.program_id(2)
    kk = pl.program_id(3)

    @pl.when(kk == 0)
    def _():
        acc_sc[...] = jnp.zeros(acc_sc.shape, F32)
        r_sc[...] = jnp.zeros(r_sc.shape, F32)

    if has_new:
        @pl.when(kk == 0)
        def _():
            _sb_step(q_ref, kn_ref, vn_ref, q_off, q_off, acc_sc, r_sc, cw, True)

        @pl.when(kk > 0)
        def _():
            _sb_step(q_ref, k_ref, v_ref, q_off, (nkc - kk) * tk, acc_sc, r_sc, cw, False)
    else:
        @pl.when(kk == 0)
        def _():
            _sb_step(q_ref, k_ref, v_ref, qi * tq, qi * tk, acc_sc, r_sc, cw, True)

        @pl.when((kk > 0) & (kk <= qi))
        def _():
            _sb_step(q_ref, k_ref, v_ref, qi * tq, (qi - kk) * tk, acc_sc, r_sc, cw, False)

    @pl.when(kk == pl.num_programs(3) - 1)
    def _():
        for hh in range(SB_PAIR):
            o_ref[:, hh * HEAD_DIM:(hh + 1) * HEAD_DIM] = acc_sc[hh].astype(o_ref.dtype)


def _sb_attention(q, k, v, layer, tq, tk, new=None):
    B, Tq, _ = q.shape
    has_new = new is not None
    hw = SB_PAIR * HEAD_DIM
    if has_new:
        nkc = k.shape[2] // tk
        nk = nkc + 1
        q_off = k.shape[2]
        kv_spec = pl.BlockSpec((None, None, tk, hw),
                               lambda b, h, qi, kk: (layer, b, jnp.clip(nkc - kk, 0, nkc - 1), h))
        tn = new[0].shape[1]
        new_specs = [pl.BlockSpec((None, tn, hw), lambda b, h, qi, kk: (b, 0, h))] * 2
        new_args = list(new)
    else:
        assert tq == tk
        nkc = k.shape[1] // tk
        nk = nkc
        q_off = 0
        kv_spec = pl.BlockSpec((None, tk, hw), lambda b, h, qi, kk: (b, jnp.maximum(qi - kk, 0), h))
        new_specs, new_args = [], []
    return pl.pallas_call(
        functools.partial(_sb_kernel, has_new=has_new, tq=tq, tk=tk, q_off=q_off, nkc=nkc, cw=MXU_TILE),
        grid=(B, SB_HEADS // SB_PAIR, Tq // tq, nk),
        in_specs=[pl.BlockSpec((None, tq, hw), lambda b, h, qi, kk: (b, qi, h)), kv_spec, kv_spec, *new_specs],
        out_specs=pl.BlockSpec((None, tq, hw), lambda b, h, qi, kk: (b, qi, h)),
        out_shape=jax.ShapeDtypeStruct((B, Tq, SELF_WIDTH), BF16),
        scratch_shapes=[pltpu.VMEM((SB_PAIR, tq, HEAD_DIM), F32), pltpu.VMEM((SB_PAIR, tq, 1), F32)],
        compiler_params=_params("parallel", "parallel", "parallel", "arbitrary"),
        name="sb_attention",
    )(q, k, v, *new_args)


def _mem_kernel(q_ref, k_ref, v_ref, o_ref):
    q = q_ref[...]
    kb = k_ref[...].astype(BF16)
    vb = v_ref[...].astype(BF16)
    for h in range(MEM_HEADS):
        sl = slice(h * HEAD_DIM, (h + 1) * HEAD_DIM)
        s = _dot_nt(q[:, sl], kb[:, sl]) * SCALE
        p = jnp.exp(s - jnp.max(s, axis=1, keepdims=True))
        o = _dot(p.astype(BF16), vb[:, sl]) / jnp.sum(p, axis=1, keepdims=True)
        o_ref[:, sl] = o.astype(o_ref.dtype)


def _mem_attention(mq, mk, mv, tq, layer=None):
    B, Tq, _ = mq.shape
    n_mem = mk.shape[-2]
    if layer is None:
        kv_spec = pl.BlockSpec((None, n_mem, MEM_WIDTH), lambda b, qi: (b, 0, 0))
    else:
        kv_spec = pl.BlockSpec((None, None, n_mem, MEM_WIDTH), lambda b, qi: (layer, b, 0, 0))
    return pl.pallas_call(
        _mem_kernel,
        grid=(B, Tq // tq),
        in_specs=[pl.BlockSpec((None, tq, MEM_WIDTH), lambda b, qi: (b, qi, 0)), kv_spec, kv_spec],
        out_specs=pl.BlockSpec((None, tq, MEM_WIDTH), lambda b, qi: (b, qi, 0)),
        out_shape=jax.ShapeDtypeStruct((B, Tq, MEM_WIDTH), BF16),
        compiler_params=_params("parallel", "parallel"),
        name="mem_attention",
    )(mq, mk, mv)


def _top_values(x, n):
    out = []
    for _ in range(n):
        m = jnp.max(x, axis=0, keepdims=True)
        out.append(m)
        x = jnp.where(x == m, -jnp.inf, x)
    return out


def _peer_select_kernel(q_ref, keys_ref, thr_ref, e1z_ref, s2_ref, e2_ref, cand_sc):
    n_top = PEER_TOPK + 1
    pairs = [(p, q) for p in range(n_top) for q in range(n_top) if (p + 1) * (q + 1) <= n_top]
    for h in range(PEER_HEADS):
        s = []
        for c in range(2):
            col = (2 * h + c) * HEAD_DIM
            s.append(_dot_nt(keys_ref[h, c], q_ref[:, col:col + HEAD_DIM]))
        a = _top_values(s[0], n_top)
        b = _top_values(s[1], n_top)
        cand_sc[...] = jnp.full(cand_sc.shape, -jnp.inf, F32)
        for r, (p, q) in enumerate(pairs):
            cand_sc[r:r + 1, :] = a[p] + b[q]
        c = _top_values(cand_sc[...], n_top)
        z = jnp.ones_like(c[0])
        for kth in range(1, PEER_TOPK):
            z = z + jnp.exp(c[kth] - c[0])
        tau = 0.5 * (c[PEER_TOPK - 1] + c[PEER_TOPK])
        thr_ref[h] = tau - s[0]
        e1z_ref[h] = jnp.exp(s[0] - a[0]) / z
        e2 = jnp.exp(s[1] - b[0])
        for lg in range(s2_ref.shape[1]):
            s2_ref[h, lg] = s[1][:, lg * LANES:(lg + 1) * LANES]
            e2_ref[h, lg] = e2[:, lg * LANES:(lg + 1) * LANES]


def _peer_select(qp, keys, layer, tt):
    T = qp.shape[0]
    tt = _tile(T, tt)
    n_pairs = sum(1 for p in range(1, PEER_TOPK + 2) for q in range(1, PEER_TOPK + 2) if p * q <= PEER_TOPK + 1)
    out = jax.ShapeDtypeStruct((PEER_HEADS, N_KEYS, T), F32)
    spec = pl.BlockSpec((PEER_HEADS, N_KEYS, tt), lambda i: (0, 0, i))
    out_lg = jax.ShapeDtypeStruct((PEER_HEADS, T // LANES, N_KEYS, LANES), F32)
    spec_lg = pl.BlockSpec((PEER_HEADS, tt // LANES, N_KEYS, LANES), lambda i: (0, i, 0, 0))
    return pl.pallas_call(
        _peer_select_kernel,
        grid=(T // tt,),
        in_specs=[pl.BlockSpec((tt, PEER_HEADS * 2 * HEAD_DIM), lambda i: (i, 0)),
                  pl.BlockSpec((None, PEER_HEADS, 2, N_KEYS, HEAD_DIM), lambda i: (layer, 0, 0, 0, 0))],
        out_specs=[spec, spec, spec_lg, spec_lg],
        out_shape=[out, out, out_lg, out_lg],
        scratch_shapes=[pltpu.VMEM((-(-n_pairs // 8) * 8, tt), F32)],
        compiler_params=_params("parallel"),
        name="peer_select",
    )(qp, keys)


def _peer_gate(thr_ref, e1z_ref, s2_ref, e2_ref, act_ref, hid_ref, rows, row0, slab, lg, sb):
    ls = slice(lg * LANES, (lg + 1) * LANES)
    js = slice(sb * slab, (sb + 1) * slab)
    gates = [None] * rows
    for h in range(PEER_HEADS):
        s2 = s2_ref[h, lg, js, :]
        e2 = e2_ref[h, lg, js, :]
        for ii in range(rows):
            thr = jnp.broadcast_to(thr_ref[h, row0 + ii:row0 + ii + 1, ls], (slab, LANES))
            e1z = jnp.broadcast_to(e1z_ref[h, row0 + ii:row0 + ii + 1, ls], (slab, LANES))
            g = jnp.where(s2 >= thr, e2 * e1z, 0.0)
            gates[ii] = g if gates[ii] is None else gates[ii] + g
    for ii in range(rows):
        es = slice(ii * N_KEYS + sb * slab, ii * N_KEYS + (sb + 1) * slab)
        a = act_ref[es, ls]
        hid_ref[es, ls] = (0.5 * a * (1.0 + lax.erf(a * (2.0 ** -0.5))) * gates[ii]).astype(BF16)


def _peer_mix_kernel(xt_ref, u_ref, vt_ref, thr_ref, e1z_ref, s2_ref, e2_ref, h_ref, g_ref, b_ref,
                     y_ref, acc_sc, act0, act1, hid0, hid1, *, rows, ne, alpha):
    f = pl.program_id(0)
    tt = xt_ref.shape[1]
    slab = GATE_ACC_VREGS * 8 // rows

    @pl.when(f == 0)
    def _():
        acc_sc[...] = jnp.zeros(acc_sc.shape, F32)
        for r in (act0, act1, hid0, hid1):
            r[...] = jnp.zeros(r.shape, r.dtype)

    def stages(act_w, act_r, hid_w, hid_r, row0):
        mt = MXU_TILE
        pieces = []
        kh = xt_ref.shape[0] // 2
        for n in range(tt // mt):
            for m in range(act_w.shape[0] // mt):
                for k in range(2):
                    def mm1(m=m, n=n, k=k):
                        d = _dot(u_ref[m * mt:(m + 1) * mt, k * kh:(k + 1) * kh],
                                 xt_ref[k * kh:(k + 1) * kh, n * mt:(n + 1) * mt])
                        if k == 0:
                            act_w[m * mt:(m + 1) * mt, n * mt:(n + 1) * mt] = d
                        else:
                            act_w[m * mt:(m + 1) * mt, n * mt:(n + 1) * mt] += d
                    pieces.append((mm1, kh // mt))
        mn = min(tt, (4 * mt * mt) // hid_r.shape[0])
        for m in range(acc_sc.shape[0] // mt):
            for n in range(tt // mn):
                def mm2(m=m, n=n):
                    acc_sc[m * mt:(m + 1) * mt, n * mn:(n + 1) * mn] += _dot(
                        vt_ref[m * mt:(m + 1) * mt, :], hid_r[:, n * mn:(n + 1) * mn])
                pieces.append((mm2, (hid_r.shape[0] // mt) * (mn // mt)))
        groups = [(lg, sb) for lg in range(tt // LANES) for sb in range(N_KEYS // slab)]
        total = sum(c for _, c in pieces)
        issued = 0
        for gi, (lg, sb) in enumerate(groups):
            while pieces and issued * len(groups) <= gi * total:
                fn, cost = pieces.pop(0)
                fn()
                issued += cost
            _peer_gate(thr_ref, e1z_ref, s2_ref, e2_ref, act_r, hid_w, rows, row0, slab, lg, sb)
        for fn, _ in pieces:
            fn()

    @pl.when(f % 2 == 0)
    def _():
        stages(act0, act1, hid1, hid0, rows % 8)

    @pl.when(f % 2 == 1)
    def _():
        stages(act1, act0, hid0, hid1, 0)

    @pl.when((f >= 2) & ((f - 2) % ne == ne - 1))
    def _():
        y_ref[...] = _layer_norm(alpha * h_ref[...] + acc_sc[...].T, g_ref[...], b_ref[...])
        acc_sc[...] = jnp.zeros(acc_sc.shape, F32)


def _peer_mix(xt, u, vt, sel, layer, h, ln_g, ln_b, alpha, tt, rows):
    D, T = xt.shape
    tt = _tile(T, tt)
    ne, eb = vt.shape[1], vt.shape[3]
    assert eb == rows * N_KEYS and ne * eb == u.shape[1] and rows in (4, 8) and ne % 2 == 0
    last = (T // tt) * ne - 1

    def tile_of(f):
        return jnp.clip(f, 0, last) // ne

    def block_of(f):
        return jnp.clip(f, 0, last) % ne

    row_spec = pl.BlockSpec((PEER_HEADS, None, 8, tt), lambda f: (0, block_of(f - 1) * rows // 8, 0, tile_of(f - 1)))
    full_spec = pl.BlockSpec((PEER_HEADS, tt // LANES, N_KEYS, LANES), lambda f: (0, tile_of(f - 1), 0, 0))
    thr, e1z, s2, e2 = sel
    thr = thr.reshape(PEER_HEADS, N_KEYS // 8, 8, T)
    e1z = e1z.reshape(PEER_HEADS, N_KEYS // 8, 8, T)
    return pl.pallas_call(
        functools.partial(_peer_mix_kernel, rows=rows, ne=ne, alpha=alpha),
        grid=(last + 3,),
        in_specs=[
            pl.BlockSpec((D, tt), lambda f: (0, tile_of(f))),
            pl.BlockSpec((None, eb, D), lambda f: (layer, block_of(f), 0)),
            pl.BlockSpec((None, None, D, eb), lambda f: (layer, block_of(f - 2), 0, 0)),
            row_spec, row_spec, full_spec, full_spec,
            pl.BlockSpec((tt, D), lambda f: (tile_of(f - 2), 0), pipeline_mode=pl.Buffered(1)),
            pl.BlockSpec((None, 1, D), lambda f: (layer, 0, 0)),
            pl.BlockSpec((None, 1, D), lambda f: (layer, 0, 0)),
        ],
        out_specs=pl.BlockSpec((tt, D), lambda f: (tile_of(f - 2), 0)),
        out_shape=jax.ShapeDtypeStruct((T, D), F32),
        scratch_shapes=[pltpu.VMEM((D, tt), F32), pltpu.VMEM((eb, tt), F32), pltpu.VMEM((eb, tt), F32),
                        pltpu.VMEM((eb, tt), BF16), pltpu.VMEM((eb, tt), BF16)],
        compiler_params=_params("arbitrary"),
        name="peer_mix",
    )(xt, u, vt, thr, e1z, s2, e2, h, ln_g, ln_b)


def kernel(x_prompt, x_sample, cache_self_k, cache_self_v, cache_mem_k, cache_mem_v, mem_prompt, w_in, w_o, w_mem_k, w_mem_v, rel_bias_table, diff_lambda, diff_subln_g, ln_g, ln_b, peer_w_q, peer_sub_keys, peer_u, peer_v):
    depth = w_in.shape[0]
    bp, n_prompt, d_model = x_prompt.shape
    bs, n_new, _ = x_sample.shape
    n_past = cache_self_k.shape[2]
    n_mem = mem_prompt.shape[1]
    alpha = (2 * depth) ** 0.25
    S = SELF_WIDTH

    w_in_b = w_in.astype(BF16)
    w_o_b = w_o.astype(BF16)
    w_mk_b = w_mem_k.astype(BF16)
    w_mv_b = w_mem_v.astype(BF16)
    w_pq_b = peer_w_q.astype(BF16)
    keys_b = peer_sub_keys.astype(BF16)
    u_b = peer_u.astype(BF16)
    eb = PEER_ROWS * N_KEYS
    vt_b = jnp.swapaxes(peer_v.reshape(depth, -1, eb, d_model), 2, 3).astype(BF16)
    ln_g4 = ln_g.reshape(depth, 2, 1, d_model)
    ln_b4 = ln_b.reshape(depth, 2, 1, d_model)
    cmk = cache_mem_k.reshape(depth, bs, n_mem, MEM_WIDTH)
    cmv = cache_mem_v.reshape(depth, bs, n_mem, MEM_WIDTH)
    mem2d = mem_prompt.reshape(bp * n_mem, d_model)

    tq_p = _tile(n_prompt, 512)
    tk_s = _tile(n_past, 512)
    nq_p = n_prompt // tq_p
    nk_s = n_past // tk_s
    i32 = jnp.int32
    bias_p = _bias_tiles(rel_bias_table, jnp.arange(nq_p, dtype=i32) * tq_p, jnp.zeros((nq_p,), i32), tq_p, tq_p)
    bias_sc = _bias_tiles(rel_bias_table, jnp.full((nk_s,), n_past, i32), jnp.arange(nk_s, dtype=i32) * tk_s, n_new, tk_s)
    bias_sn = _bias_tiles(rel_bias_table, jnp.full((1,), n_past, i32), jnp.full((1,), n_past, i32), n_new, n_new)

    hp = x_prompt.reshape(bp * n_prompt, d_model)
    hs = x_sample.reshape(bs * n_new, d_model)
    segs = ((0, S), (S, S), (2 * S, S), (3 * S, MEM_WIDTH))
    seg_dt = (BF16, F32, F32, BF16)
    new_k_p, new_v_p, new_mk_p, new_mv_p, new_k_s, new_v_s = [], [], [], [], [], []
    for i in range(depth):
        kind = i % 2
        j = i // 2
        qp, kp, vp, mqp = _linear(hp, w_in_b, i, segs, seg_dt, 256)
        qs, ks, vs, mqs = _linear(hs, w_in_b, i, segs, seg_dt, 256)
        (mkp,) = _linear(mem2d, w_mk_b, i, ((0, MEM_WIDTH),), (F32,), 512)
        (mvp,) = _linear(mem2d, w_mv_b, i, ((0, MEM_WIDTH),), (F32,), 512)
        qp3, kp3, vp3 = (a.reshape(bp, n_prompt, S) for a in (qp, kp, vp))
        qs3, ks3, vs3 = (a.reshape(bs, n_new, S) for a in (qs, ks, vs))

        if kind == 0:
            g = diff_subln_g[j].reshape(1, DIFF_VDIM)
            op = _diff_attention(qp3, kp3, vp3, bias_p, diff_lambda[j], g, i, tq_p, tq_p)
            os_ = _diff_attention(qs3, cache_self_k, cache_self_v, bias_sc, diff_lambda[j], g, i, n_new, tk_s,
                                  new=(ks3, vs3, bias_sn))
        else:
            op = _sb_attention(qp3, kp3, vp3, i, tq_p, tq_p)
            os_ = _sb_attention(qs3, cache_self_k, cache_self_v, i, n_new, tk_s, new=(ks3, vs3))

        mop = _mem_attention(mqp.reshape(bp, n_prompt, MEM_WIDTH), mkp.reshape(bp, n_mem, MEM_WIDTH),
                             mvp.reshape(bp, n_mem, MEM_WIDTH), tq_p)
        mos = _mem_attention(mqs.reshape(bs, n_new, MEM_WIDTH), cmk, cmv, n_new, layer=i)

        outs = []
        for h_res, o_self, o_mem in ((hp, op, mop), (hs, os_, mos)):
            T = h_res.shape[0]
            y, yt = _out_proj(o_self.reshape(T, S), o_mem.reshape(T, MEM_WIDTH), w_o_b, i, h_res,
                              ln_g4[:, 0], ln_b4[:, 0], alpha, 256)
            (pq,) = _linear(y, w_pq_b, i, ((0, PEER_HEADS * 2 * HEAD_DIM),), (BF16,), 256)
            sel = _peer_select(pq, keys_b, i, 512)
            outs.append(_peer_mix(yt, u_b, vt_b, sel, i, y, ln_g4[:, 1], ln_b4[:, 1], alpha, 512, PEER_ROWS))
        hp, hs = outs

        new_k_p.append(kp3)
        new_v_p.append(vp3)
        new_mk_p.append(mkp.reshape(bp, n_mem, MEM_HEADS, HEAD_DIM))
        new_mv_p.append(mvp.reshape(bp, n_mem, MEM_HEADS, HEAD_DIM))
        new_k_s.append(ks3)
        new_v_s.append(vs3)

    return (hp.reshape(bp, n_prompt, d_model), hs.reshape(bs, n_new, d_model),
            jnp.stack(new_k_p), jnp.stack(new_v_p), jnp.stack(new_mk_p), jnp.stack(new_mv_p),
            jnp.stack(new_k_s), jnp.stack(new_v_s))
```

```python
import functools
import math

import jax
import jax.numpy as jnp
from jax import lax
from jax.experimental import pallas as pl
from jax.experimental.pallas import tpu as pltpu

F32 = jnp.float32
BF16 = jnp.bfloat16

HEAD_DIM = 128
SELF_WIDTH = 1536
DIFF_HEADS = 6
DIFF_VDIM = 2 * HEAD_DIM
SB_HEADS = 12
MEM_HEADS = 4
MEM_WIDTH = MEM_HEADS * HEAD_DIM
CHUNK = 64
N_BUCKETS = 32
PEER_HEADS = 8
N_KEYS = 128
PEER_TOPK = 16
LN_EPS = 1e-5
SCALE = HEAD_DIM ** -0.5
NEG = -1e30

V7X_VMEM_LIMIT = 56 * 1024 * 1024
LANES = 128
MXU_TILE = 256
ROW_TILE = 512
SB_PAIR = 2
PEER_MM_ROWS = 512
PEER_ROWS = 4
GATE_ACC_VREGS = 16


def _params(*sem):
    return pltpu.CompilerParams(dimension_semantics=sem, vmem_limit_bytes=V7X_VMEM_LIMIT)


def _tile(n, t):
    t = min(n, t)
    assert n % t == 0, (n, t)
    return t


def _dot(a, b):
    return jnp.dot(a, b, preferred_element_type=F32)


def _dot_nt(a, b):
    return lax.dot_general(a, b, (((1,), (1,)), ((), ())), preferred_element_type=F32)


def _layer_norm(r, g, b):
    mu = jnp.mean(r, axis=-1, keepdims=True)
    d = r - mu
    var = jnp.mean(d * d, axis=-1, keepdims=True)
    return d * lax.rsqrt(var + LN_EPS) * g + b


def _linear_kernel(x_ref, w_ref, *o_refs, segs, nc):
    x = x_ref[...].astype(BF16)
    for o_ref, (start, width) in zip(o_refs, segs):
        for c in range(0, width, nc):
            o_ref[:, c:c + nc] = _dot(x, w_ref[:, start + c:start + c + nc]).astype(o_ref.dtype)


def _linear(x, w, layer, segs, dtypes, tm):
    M, K = x.shape
    N = w.shape[2]
    tm = _tile(M, tm)
    nc = 512
    assert all(wd % nc == 0 for _, wd in segs)
    return pl.pallas_call(
        functools.partial(_linear_kernel, segs=segs, nc=nc),
        grid=(M // tm,),
        in_specs=[
            pl.BlockSpec((tm, K), lambda i: (i, 0)),
            pl.BlockSpec((None, K, N), lambda i: (layer, 0, 0), pipeline_mode=pl.Buffered(1)),
        ],
        out_specs=[pl.BlockSpec((tm, wd), lambda i: (i, 0)) for _, wd in segs],
        out_shape=[jax.ShapeDtypeStruct((M, wd), dt) for (_, wd), dt in zip(segs, dtypes)],
        compiler_params=_params("parallel"),
        name="linear",
    )(x, w)


def _out_proj_kernel(o_ref, mo_ref, w_ref, h_ref, g_ref, b_ref, y_ref, yt_ref, *, alpha):
    acc = _dot(o_ref[...], w_ref[:SELF_WIDTH, :]) + _dot(mo_ref[...], w_ref[SELF_WIDTH:, :])
    y = _layer_norm(alpha * h_ref[...] + acc, g_ref[...], b_ref[...])
    y_ref[...] = y
    yt_ref[...] = y.T.astype(BF16)


def _out_proj(o, mo, w_o, layer, h, ln_g, ln_b, alpha, tm):
    T, D = h.shape
    tm = _tile(T, tm)
    return pl.pallas_call(
        functools.partial(_out_proj_kernel, alpha=alpha),
        grid=(T // tm,),
        in_specs=[
            pl.BlockSpec((tm, SELF_WIDTH), lambda i: (i, 0)),
            pl.BlockSpec((tm, MEM_WIDTH), lambda i: (i, 0)),
            pl.BlockSpec((None, SELF_WIDTH + MEM_WIDTH, D), lambda i: (layer, 0, 0), pipeline_mode=pl.Buffered(1)),
            pl.BlockSpec((tm, D), lambda i: (i, 0)),
            pl.BlockSpec((None, 1, D), lambda i: (layer, 0, 0)),
            pl.BlockSpec((None, 1, D), lambda i: (layer, 0, 0)),
        ],
        out_specs=[pl.BlockSpec((tm, D), lambda i: (i, 0)), pl.BlockSpec((D, tm), lambda i: (0, i))],
        out_shape=[jax.ShapeDtypeStruct((T, D), F32), jax.ShapeDtypeStruct((D, T), BF16)],
        compiler_params=_params("parallel"),
        name="out_proj_ln",
    )(o, mo, w_o, h, ln_g, ln_b)


def _bias_kernel(q0_ref, k0_ref, tab_ref, o_ref, *, tq, tk):
    h = pl.program_id(0)
    t = pl.program_id(1)
    q_pos = q0_ref[t] + lax.broadcasted_iota(jnp.int32, (tq, tk), 0)
    k_pos = k0_ref[t] + lax.broadcasted_iota(jnp.int32, (tq, tk), 1)
    rel = k_pos - q_pos
    n = jnp.abs(rel)
    large = jnp.full((tq, tk), 8, jnp.int32)
    for thr in (12, 16, 23, 32, 46, 64, 91):
        large = large + (n >= thr).astype(jnp.int32)
    bucket = jnp.where(rel > 0, N_BUCKETS // 2, 0) + jnp.where(n < 8, n, large)
    bias = jnp.zeros((tq, tk), F32)
    for b in range(N_BUCKETS):
        bias = jnp.where(bucket == b, tab_ref[b, h], bias)
    visible = (k_pos // CHUNK) <= (q_pos // CHUNK)
    o_ref[...] = jnp.where(visible, bias, NEG)


def _bias_tiles(rel_table, q0, k0, tq, tk):
    n = q0.shape[0]
    return pl.pallas_call(
        functools.partial(_bias_kernel, tq=tq, tk=tk),
        grid_spec=pltpu.PrefetchScalarGridSpec(
            num_scalar_prefetch=2,
            grid=(DIFF_HEADS, n),
            in_specs=[pl.BlockSpec(memory_space=pltpu.SMEM)],
            out_specs=pl.BlockSpec((None, None, tq, tk), lambda h, t, *_: (h, t, 0, 0)),
        ),
        out_shape=jax.ShapeDtypeStruct((DIFF_HEADS, n, tq, tk), F32),
        compiler_params=_params("parallel", "parallel"),
        name="bias_tiles",
    )(q0, k0, rel_table)


def _diff_step(q_ref, k_ref, v_ref, bias_ref, m_sc, l_sc, acc_sc):
    tq, tk = bias_ref.shape
    cw = min(tk, LANES)
    kb = k_ref[...].astype(BF16)
    v_ext = jnp.concatenate([v_ref[...].astype(BF16), jnp.ones((tk, LANES), BF16)], axis=1)
    live = [dict(), dict()]

    def scores(c):
        cs = slice(c * HEAD_DIM, (c + 1) * HEAD_DIM)
        live[c]["s"] = _dot_nt(q_ref[:, cs], kb[:, cs]) * SCALE + bias_ref[...]

    def probs(c):
        s = live[c].pop("s")
        m_prev = m_sc[c]
        m_new = jnp.maximum(m_prev, jnp.broadcast_to(jnp.max(s, axis=1, keepdims=True), (tq, LANES)))
        live[c]["alpha"] = jnp.exp(m_prev - m_new)
        m_sc[c] = m_new
        p = [jnp.exp(s[:, j:j + cw] - m_new[:, :cw]) for j in range(0, tk, cw)]
        live[c]["p"] = jnp.concatenate(p, axis=1).astype(BF16)

    def values(c):
        live[c]["pv"] = _dot(live[c].pop("p"), v_ext)

    def update(c):
        alpha = live[c].pop("alpha")
        pv = live[c].pop("pv")
        acc_sc[c] = jnp.concatenate([alpha, alpha], axis=1) * acc_sc[c] + pv[:, :DIFF_VDIM]
        l_sc[c] = alpha * l_sc[c] + pv[:, DIFF_VDIM:]

    stages = (scores, probs, values, update)
    for t in range(2 + len(stages) - 1):
        for s in reversed(range(len(stages))):
            if 0 <= t - s < 2:
                stages[s](t - s)


def _diff_kernel(*refs, has_new, lam_init):
    if has_new:
        lam_ref, g_ref, q_ref, k_ref, v_ref, bias_ref, kn_ref, vn_ref, biasn_ref, o_ref, m_sc, l_sc, acc_sc = refs
    else:
        lam_ref, g_ref, q_ref, k_ref, v_ref, bias_ref, o_ref, m_sc, l_sc, acc_sc = refs
    qi = pl.program_id(2)
    kk = pl.program_id(3)

    @pl.when(kk == 0)
    def _():
        m_sc[...] = jnp.full(m_sc.shape, NEG, F32)
        l_sc[...] = jnp.zeros(l_sc.shape, F32)
        acc_sc[...] = jnp.zeros(acc_sc.shape, F32)

    if has_new:
        @pl.when(kk == 0)
        def _():
            _diff_step(q_ref, kn_ref, vn_ref, biasn_ref, m_sc, l_sc, acc_sc)

        @pl.when(kk > 0)
        def _():
            _diff_step(q_ref, k_ref, v_ref, bias_ref, m_sc, l_sc, acc_sc)
    else:
        @pl.when(kk <= qi)
        def _():
            _diff_step(q_ref, k_ref, v_ref, bias_ref, m_sc, l_sc, acc_sc)

    @pl.when(kk == pl.num_programs(3) - 1)
    def _():
        lp = lam_ref[...]
        lam = (jnp.exp(jnp.sum(lp[0:1] * lp[1:2], axis=1, keepdims=True))
               - jnp.exp(jnp.sum(lp[2:3] * lp[3:4], axis=1, keepdims=True)) + lam_init)
        l0 = jnp.concatenate([l_sc[0], l_sc[0]], axis=1)
        l1 = jnp.concatenate([l_sc[1], l_sc[1]], axis=1)
        o = acc_sc[0] / l0 - lam * (acc_sc[1] / l1)
        o = o * lax.rsqrt(jnp.mean(o * o, axis=1, keepdims=True) + LN_EPS) * g_ref[...]
        o_ref[...] = (o * (1.0 - lam_init)).astype(o_ref.dtype)


def _diff_attention(q, k, v, bias, lam_vec, subln_g, layer, tq, tk, new=None):
    B, Tq, _ = q.shape
    lam_init = 0.8 - 0.6 * math.exp(-0.3 * layer)
    has_new = new is not None
    if has_new:
        nkc = k.shape[2] // tk
        nk = nkc + 1
        kv_spec = pl.BlockSpec((None, None, tk, DIFF_VDIM),
                               lambda b, h, qi, kk: (layer, b, jnp.clip(nkc - kk, 0, nkc - 1), h))
        bias_spec = pl.BlockSpec((None, None, tq, tk), lambda b, h, qi, kk: (h, jnp.clip(nkc - kk, 0, nkc - 1), 0, 0))
        tn = new[0].shape[1]
        new_specs = [pl.BlockSpec((None, tn, DIFF_VDIM), lambda b, h, qi, kk: (b, 0, h)),
                     pl.BlockSpec((None, tn, DIFF_VDIM), lambda b, h, qi, kk: (b, 0, h)),
                     pl.BlockSpec((None, None, tq, tn), lambda b, h, qi, kk: (h, 0, 0, 0))]
        new_args = list(new)
    else:
        nk = k.shape[1] // tk
        kv_spec = pl.BlockSpec((None, tk, DIFF_VDIM), lambda b, h, qi, kk: (b, jnp.maximum(qi - kk, 0), h))
        bias_spec = pl.BlockSpec((None, None, tq, tk), lambda b, h, qi, kk: (h, jnp.minimum(kk, qi), 0, 0))
        new_specs, new_args = [], []
    return pl.pallas_call(
        functools.partial(_diff_kernel, has_new=has_new, lam_init=lam_init),
        grid=(B, DIFF_HEADS, Tq // tq, nk),
        in_specs=[
            pl.BlockSpec((4, HEAD_DIM), lambda b, h, qi, kk: (0, 0)),
            pl.BlockSpec((1, DIFF_VDIM), lambda b, h, qi, kk: (0, 0)),
            pl.BlockSpec((None, tq, DIFF_VDIM), lambda b, h, qi, kk: (b, qi, h)),
            kv_spec, kv_spec, bias_spec, *new_specs,
        ],
        out_specs=pl.BlockSpec((None, tq, DIFF_VDIM), lambda b, h, qi, kk: (b, qi, h)),
        out_shape=jax.ShapeDtypeStruct((B, Tq, SELF_WIDTH), BF16),
        scratch_shapes=[pltpu.VMEM((2, tq, LANES), F32), pltpu.VMEM((2, tq, LANES), F32),
                        pltpu.VMEM((2, tq, DIFF_VDIM), F32)],
        compiler_params=_params("parallel", "parallel", "parallel", "arbitrary"),
        name="diff_attention",
    )(lam_vec, subln_g, q, k, v, bias, *new_args)


def _sb_step(q_ref, k_ref, v_ref, q_pos0, k_pos0, acc_sc, r_sc, cw, masked):
    tq = q_ref.shape[0]
    tk = k_ref.shape[0]
    cw = min(cw, tk)
    later = (lax.broadcasted_iota(jnp.int32, (cw, cw), 0) > lax.broadcasted_iota(jnp.int32, (cw, cw), 1))
    later = jnp.where(later, 1.0, 0.0).astype(BF16)
    units = [(c0, hh) for c0 in range(tk - cw, -1, -cw) for hh in range(SB_PAIR)]
    live = [dict() for _ in units]

    def scores(u):
        c0, hh = units[u]
        hs = slice(hh * HEAD_DIM, (hh + 1) * HEAD_DIM)
        live[u]["z"] = _dot_nt(q_ref[:, hs], k_ref[c0:c0 + cw, hs].astype(BF16)) * SCALE

    def logs(u):
        c0, _ = units[u]
        z = live[u].pop("z")
        log_keep = -(jnp.maximum(z, 0.0) + jnp.log(1.0 + jnp.exp(-jnp.abs(z))))
        live[u]["log_beta"] = log_keep + z
        if masked:
            q_pos = q_pos0 + lax.broadcasted_iota(jnp.int32, (tq, cw), 0)
            before = (k_pos0 + c0 + lax.broadcasted_iota(jnp.int32, (tq, cw), 1)) < q_pos
            log_keep = jnp.where(before, log_keep, 0.0)
            live[u]["before"] = before
        live[u]["log_keep"] = log_keep

    def cumsum(u):
        live[u]["cum"] = _dot(live[u]["log_keep"].astype(BF16), later)

    def weights(u):
        _, hh = units[u]
        a = jnp.exp(live[u].pop("log_beta") + (live[u].pop("cum") + r_sc[hh]))
        if masked:
            a = jnp.where(live[u].pop("before"), a, 0.0)
        live[u]["a"] = a.astype(BF16)
        r_sc[hh] += jnp.sum(live[u].pop("log_keep"), axis=1, keepdims=True)

    def values(u):
        c0, hh = units[u]
        hs = slice(hh * HEAD_DIM, (hh + 1) * HEAD_DIM)
        acc_sc[hh] += _dot(live[u].pop("a"), v_ref[c0:c0 + cw, hs].astype(BF16))

    stages = (scores, logs, cumsum, weights, values)
    for t in range(len(units) + len(stages) - 1):
        for s in reversed(range(len(stages))):
            if 0 <= t - s < len(units):
                stages[s](t - s)


def _sb_kernel(*refs, has_new, tq, tk, q_off, nkc, cw):
    if has_new:
        q_ref, k_ref, v_ref, kn_ref, vn_ref, o_ref, acc_sc, r_sc = refs
    else:
        q_ref, k_ref, v_ref, o_ref, acc_sc, r_sc = refs
    qi = pl.program_id(2)
    kk = pl.program_id(3)

    @pl.when(kk == 0)
    def _():
        acc_sc[...] = jnp.zeros(acc_sc.shape, F32)
        r_sc[...] = jnp.zeros(r_sc.shape, F32)

    if has_new:
        @pl.when(kk == 0)
        def _():
            _sb_step(q_ref, kn_ref, vn_ref, q_off, q_off, acc_sc, r_sc, cw, True)

        @pl.when(kk > 0)
        def _():
            _sb_step(q_ref, k_ref, v_ref, q_off, (nkc - kk) * tk, acc_sc, r_sc, cw, False)
    else:
        @pl.when(kk == 0)
        def _():
            _sb_step(q_ref, k_ref, v_ref, qi * tq, qi * tk, acc_sc, r_sc, cw, True)

        @pl.when((kk > 0) & (kk <= qi))
        def _():
            _sb_step(q_ref, k_ref, v_ref, qi * tq, (qi - kk) * tk, acc_sc, r_sc, cw, False)

    @pl.when(kk == pl.num_programs(3) - 1)
    def _():
        for hh in range(SB_PAIR):
            o_ref[:, hh * HEAD_DIM:(hh + 1) * HEAD_DIM] = acc_sc[hh].astype(o_ref.dtype)


def _sb_attention(q, k, v, layer, tq, tk, new=None):
    B, Tq, _ = q.shape
    has_new = new is not None
    hw = SB_PAIR * HEAD_DIM
    if has_new:
        nkc = k.shape[2] // tk
        nk = nkc + 1
        q_off = k.shape[2]
        kv_spec = pl.BlockSpec((None, None, tk, hw),
                               lambda b, h, qi, kk: (layer, b, jnp.clip(nkc - kk, 0, nkc - 1), h))
        tn = new[0].shape[1]
        new_specs = [pl.BlockSpec((None, tn, hw), lambda b, h, qi, kk: (b, 0, h))] * 2
        new_args = list(new)
    else:
        assert tq == tk
        nkc = k.shape[1] // tk
        nk = nkc
        q_off = 0
        kv_spec = pl.BlockSpec((None, tk, hw), lambda b, h, qi, kk: (b, jnp.maximum(qi - kk, 0), h))
        new_specs, new_args = [], []
    return pl.pallas_call(
        functools.partial(_sb_kernel, has_new=has_new, tq=tq, tk=tk, q_off=q_off, nkc=nkc, cw=MXU_TILE),
        grid=(B, SB_HEADS // SB_PAIR, Tq // tq, nk),
        in_specs=[pl.BlockSpec((None, tq, hw), lambda b, h, qi, kk: (b, qi, h)), kv_spec, kv_spec, *new_specs],
        out_specs=pl.BlockSpec((None, tq, hw), lambda b, h, qi, kk: (b, qi, h)),
        out_shape=jax.ShapeDtypeStruct((B, Tq, SELF_WIDTH), BF16),
        scratch_shapes=[pltpu.VMEM((SB_PAIR, tq, HEAD_DIM), F32), pltpu.VMEM((SB_PAIR, tq, 1), F32)],
        compiler_params=_params("parallel", "parallel", "parallel", "arbitrary"),
        name="sb_attention",
    )(q, k, v, *new_args)


def _mem_kernel(q_ref, k_ref, v_ref, o_ref):
    q = q_ref[...]
    kb = k_ref[...].astype(BF16)
    vb = v_ref[...].astype(BF16)
    for h in range(MEM_HEADS):
        sl = slice(h * HEAD_DIM, (h + 1) * HEAD_DIM)
        s = _dot_nt(q[:, sl], kb[:, sl]) * SCALE
        p = jnp.exp(s - jnp.max(s, axis=1, keepdims=True))
        o = _dot(p.astype(BF16), vb[:, sl]) / jnp.sum(p, axis=1, keepdims=True)
        o_ref[:, sl] = o.astype(o_ref.dtype)


def _mem_attention(mq, mk, mv, tq, layer=None):
    B, Tq, _ = mq.shape
    n_mem = mk.shape[-2]
    if layer is None:
        kv_spec = pl.BlockSpec((None, n_mem, MEM_WIDTH), lambda b, qi: (b, 0, 0))
    else:
        kv_spec = pl.BlockSpec((None, None, n_mem, MEM_WIDTH), lambda b, qi: (layer, b, 0, 0))
    return pl.pallas_call(
        _mem_kernel,
        grid=(B, Tq // tq),
        in_specs=[pl.BlockSpec((None, tq, MEM_WIDTH), lambda b, qi: (b, qi, 0)), kv_spec, kv_spec],
        out_specs=pl.BlockSpec((None, tq, MEM_WIDTH), lambda b, qi: (b, qi, 0)),
        out_shape=jax.ShapeDtypeStruct((B, Tq, MEM_WIDTH), BF16),
        compiler_params=_params("parallel", "parallel"),
        name="mem_attention",
    )(mq, mk, mv)


def _top_values(x, n):
    out = []
    for _ in range(n):
        m = jnp.max(x, axis=0, keepdims=True)
        out.append(m)
        x = jnp.where(x == m, -jnp.inf, x)
    return out


def _peer_select_kernel(q_ref, keys_ref, thr_ref, e1z_ref, s2_ref, e2_ref, cand_sc):
    n_top = PEER_TOPK + 1
    pairs = [(p, q) for p in range(n_top) for q in range(n_top) if (p + 1) * (q + 1) <= n_top]
    for h in range(PEER_HEADS):
        s = []
        for c in range(2):
            col = (2 * h + c) * HEAD_DIM
            s.append(_dot_nt(keys_ref[h, c], q_ref[:, col:col + HEAD_DIM]))
        a = _top_values(s[0], n_top)
        b = _top_values(s[1], n_top)
        cand_sc[...] = jnp.full(cand_sc.shape, -jnp.inf, F32)
        for r, (p, q) in enumerate(pairs):
            cand_sc[r:r + 1, :] = a[p] + b[q]
        c = _top_values(cand_sc[...], n_top)
        z = jnp.ones_like(c[0])
        for kth in range(1, PEER_TOPK):
            z = z + jnp.exp(c[kth] - c[0])
        tau = 0.5 * (c[PEER_TOPK - 1] + c[PEER_TOPK])
        thr_ref[h] = tau - s[0]
        e1z_ref[h] = jnp.exp(s[0] - a[0]) / z
        e2 = jnp.exp(s[1] - b[0])
        for lg in range(s2_ref.shape[1]):
            s2_ref[h, lg] = s[1][:, lg * LANES:(lg + 1) * LANES]
            e2_ref[h, lg] = e2[:, lg * LANES:(lg + 1) * LANES]


def _peer_select(qp, keys, layer, tt):
    T = qp.shape[0]
    tt = _tile(T, tt)
    n_pairs = sum(1 for p in range(1, PEER_TOPK + 2) for q in range(1, PEER_TOPK + 2) if p * q <= PEER_TOPK + 1)
    out = jax.ShapeDtypeStruct((PEER_HEADS, N_KEYS, T), F32)
    spec = pl.BlockSpec((PEER_HEADS, N_KEYS, tt), lambda i: (0, 0, i))
    out_lg = jax.ShapeDtypeStruct((PEER_HEADS, T // LANES, N_KEYS, LANES), F32)
    spec_lg = pl.BlockSpec((PEER_HEADS, tt // LANES, N_KEYS, LANES), lambda i: (0, i, 0, 0))
    return pl.pallas_call(
        _peer_select_kernel,
        grid=(T // tt,),
        in_specs=[pl.BlockSpec((tt, PEER_HEADS * 2 * HEAD_DIM), lambda i: (i, 0)),
                  pl.BlockSpec((None, PEER_HEADS, 2, N_KEYS, HEAD_DIM), lambda i: (layer, 0, 0, 0, 0))],
        out_specs=[spec, spec, spec_lg, spec_lg],
        out_shape=[out, out, out_lg, out_lg],
        scratch_shapes=[pltpu.VMEM((-(-n_pairs // 8) * 8, tt), F32)],
        compiler_params=_params("parallel"),
        name="peer_select",
    )(qp, keys)


def _peer_gate(thr_ref, e1z_ref, s2_ref, e2_ref, act_ref, hid_ref, rows, row0, slab, lg, sb):
    ls = slice(lg * LANES, (lg + 1) * LANES)
    js = slice(sb * slab, (sb + 1) * slab)
    gates = [None] * rows
    for h in range(PEER_HEADS):
        s2 = s2_ref[h, lg, js, :]
        e2 = e2_ref[h, lg, js, :]
        for ii in range(rows):
            thr = jnp.broadcast_to(thr_ref[h, row0 + ii:row0 + ii + 1, ls], (slab, LANES))
            e1z = jnp.broadcast_to(e1z_ref[h, row0 + ii:row0 + ii + 1, ls], (slab, LANES))
            g = jnp.where(s2 >= thr, e2 * e1z, 0.0)
            gates[ii] = g if gates[ii] is None else gates[ii] + g
    for ii in range(rows):
        es = slice(ii * N_KEYS + sb * slab, ii * N_KEYS + (sb + 1) * slab)
        a = act_ref[es, ls]
        hid_ref[es, ls] = (0.5 * a * (1.0 + lax.erf(a * (2.0 ** -0.5))) * gates[ii]).astype(BF16)


def _peer_mix_kernel(xt_ref, u_ref, vt_ref, thr_ref, e1z_ref, s2_ref, e2_ref, h_ref, g_ref, b_ref,
                     y_ref, acc_sc, act0, act1, hid0, hid1, *, rows, ne, alpha):
    f = pl.program_id(0)
    tt = xt_ref.shape[1]
    slab = GATE_ACC_VREGS * 8 // rows

    @pl.when(f == 0)
    def _():
        acc_sc[...] = jnp.zeros(acc_sc.shape, F32)
        for r in (act0, act1, hid0, hid1):
            r[...] = jnp.zeros(r.shape, r.dtype)

    def stages(act_w, act_r, hid_w, hid_r, row0):
        mt = MXU_TILE
        pieces = []
        kh = xt_ref.shape[0] // 2
        m1 = min(act_w.shape[0], PEER_MM_ROWS)
        for n in range(tt // mt):
            for m in range(act_w.shape[0] // m1):
                for k in range(2):
                    def mm1(m=m, n=n, k=k):
                        d = _dot(u_ref[m * m1:(m + 1) * m1, k * kh:(k + 1) * kh],
                                 xt_ref[k * kh:(k + 1) * kh, n * mt:(n + 1) * mt])
                        if k == 0:
                            act_w[m * m1:(m + 1) * m1, n * mt:(n + 1) * mt] = d
                        else:
                            act_w[m * m1:(m + 1) * m1, n * mt:(n + 1) * mt] += d
                    pieces.append((mm1, (kh // mt) * (m1 // mt)))
        m2 = min(acc_sc.shape[0], PEER_MM_ROWS)
        for m in range(acc_sc.shape[0] // m2):
            for n in range(tt // mt):
                def mm2(m=m, n=n):
                    acc_sc[m * m2:(m + 1) * m2, n * mt:(n + 1) * mt] += _dot(
                        vt_ref[m * m2:(m + 1) * m2, :], hid_r[:, n * mt:(n + 1) * mt])
                pieces.append((mm2, (hid_r.shape[0] // mt) * (m2 // mt)))
        groups = [(lg, sb) for lg in range(tt // LANES) for sb in range(N_KEYS // slab)]
        total = sum(c for _, c in pieces)
        issued = 0
        for gi, (lg, sb) in enumerate(groups):
            while pieces and issued * len(groups) <= gi * total:
                fn, cost = pieces.pop(0)
                fn()
                issued += cost
            _peer_gate(thr_ref, e1z_ref, s2_ref, e2_ref, act_r, hid_w, rows, row0, slab, lg, sb)
        for fn, _ in pieces:
            fn()

    @pl.when(f % 2 == 0)
    def _():
        stages(act0, act1, hid1, hid0, rows % 8)

    @pl.when(f % 2 == 1)
    def _():
        stages(act1, act0, hid0, hid1, 0)

    @pl.when((f >= 2) & ((f - 2) % ne == ne - 1))
    def _():
        y_ref[...] = _layer_norm(alpha * h_ref[...] + acc_sc[...].T, g_ref[...], b_ref[...])
        acc_sc[...] = jnp.zeros(acc_sc.shape, F32)


def _peer_mix(xt, u, vt, sel, layer, h, ln_g, ln_b, alpha, tt, rows):
    D, T = xt.shape
    tt = _tile(T, tt)
    ne, eb = vt.shape[1], vt.shape[3]
    assert eb == rows * N_KEYS and ne * eb == u.shape[1] and rows in (4, 8) and ne % 2 == 0
    last = (T // tt) * ne - 1

    def tile_of(f):
        return jnp.clip(f, 0, last) // ne

    def block_of(f):
        return jnp.clip(f, 0, last) % ne

    row_spec = pl.BlockSpec((PEER_HEADS, None, 8, tt), lambda f: (0, block_of(f - 1) * rows // 8, 0, tile_of(f - 1)))
    full_spec = pl.BlockSpec((PEER_HEADS, tt // LANES, N_KEYS, LANES), lambda f: (0, tile_of(f - 1), 0, 0))
    thr, e1z, s2, e2 = sel
    thr = thr.reshape(PEER_HEADS, N_KEYS // 8, 8, T)
    e1z = e1z.reshape(PEER_HEADS, N_KEYS // 8, 8, T)
    return pl.pallas_call(
        functools.partial(_peer_mix_kernel, rows=rows, ne=ne, alpha=alpha),
        grid=(last + 3,),
        in_specs=[
            pl.BlockSpec((D, tt), lambda f: (0, tile_of(f))),
            pl.BlockSpec((None, eb, D), lambda f: (layer, block_of(f), 0)),
            pl.BlockSpec((None, None, D, eb), lambda f: (layer, block_of(f - 2), 0, 0)),
            row_spec, row_spec, full_spec, full_spec,
            pl.BlockSpec((tt, D), lambda f: (tile_of(f - 2), 0), pipeline_mode=pl.Buffered(1)),
            pl.BlockSpec((None, 1, D), lambda f: (layer, 0, 0)),
            pl.BlockSpec((None, 1, D), lambda f: (layer, 0, 0)),
        ],
        out_specs=pl.BlockSpec((tt, D), lambda f: (tile_of(f - 2), 0)),
        out_shape=jax.ShapeDtypeStruct((T, D), F32),
        scratch_shapes=[pltpu.VMEM((D, tt), F32), pltpu.VMEM((eb, tt), F32), pltpu.VMEM((eb, tt), F32),
                        pltpu.VMEM((eb, tt), BF16), pltpu.VMEM((eb, tt), BF16)],
        compiler_params=_params("arbitrary"),
        name="peer_mix",
    )(xt, u, vt, thr, e1z, s2, e2, h, ln_g, ln_b)


def kernel(x_prompt, x_sample, cache_self_k, cache_self_v, cache_mem_k, cache_mem_v, mem_prompt, w_in, w_o, w_mem_k, w_mem_v, rel_bias_table, diff_lambda, diff_subln_g, ln_g, ln_b, peer_w_q, peer_sub_keys, peer_u, peer_v):
    depth = w_in.shape[0]
    bp, n_prompt, d_model = x_prompt.shape
    bs, n_new, _ = x_sample.shape
    n_past = cache_self_k.shape[2]
    n_mem = mem_prompt.shape[1]
    alpha = (2 * depth) ** 0.25
    S = SELF_WIDTH

    w_in_b = w_in.astype(BF16)
    w_o_b = w_o.astype(BF16)
    w_mk_b = w_mem_k.astype(BF16)
    w_mv_b = w_mem_v.astype(BF16)
    w_pq_b = peer_w_q.astype(BF16)
    keys_b = peer_sub_keys.astype(BF16)
    u_b = peer_u.astype(BF16)
    eb = PEER_ROWS * N_KEYS
    vt_b = jnp.swapaxes(peer_v.reshape(depth, -1, eb, d_model), 2, 3).astype(BF16)
    ln_g4 = ln_g.reshape(depth, 2, 1, d_model)
    ln_b4 = ln_b.reshape(depth, 2, 1, d_model)
    cmk = cache_mem_k.reshape(depth, bs, n_mem, MEM_WIDTH)
    cmv = cache_mem_v.reshape(depth, bs, n_mem, MEM_WIDTH)
    mem2d = mem_prompt.reshape(bp * n_mem, d_model)

    tq_p = _tile(n_prompt, 512)
    tk_s = _tile(n_past, 512)
    nq_p = n_prompt // tq_p
    nk_s = n_past // tk_s
    i32 = jnp.int32
    bias_p = _bias_tiles(rel_bias_table, jnp.arange(nq_p, dtype=i32) * tq_p, jnp.zeros((nq_p,), i32), tq_p, tq_p)
    bias_sc = _bias_tiles(rel_bias_table, jnp.full((nk_s,), n_past, i32), jnp.arange(nk_s, dtype=i32) * tk_s, n_new, tk_s)
    bias_sn = _bias_tiles(rel_bias_table, jnp.full((1,), n_past, i32), jnp.full((1,), n_past, i32), n_new, n_new)

    hp = x_prompt.reshape(bp * n_prompt, d_model)
    hs = x_sample.reshape(bs * n_new, d_model)
    segs = ((0, S), (S, S), (2 * S, S), (3 * S, MEM_WIDTH))
    seg_dt = (BF16, F32, F32, BF16)
    new_k_p, new_v_p, new_mk_p, new_mv_p, new_k_s, new_v_s = [], [], [], [], [], []
    for i in range(depth):
        kind = i % 2
        j = i // 2
        qp, kp, vp, mqp = _linear(hp, w_in_b, i, segs, seg_dt, ROW_TILE)
        qs, ks, vs, mqs = _linear(hs, w_in_b, i, segs, seg_dt, ROW_TILE)
        (mkp,) = _linear(mem2d, w_mk_b, i, ((0, MEM_WIDTH),), (F32,), ROW_TILE)
        (mvp,) = _linear(mem2d, w_mv_b, i, ((0, MEM_WIDTH),), (F32,), ROW_TILE)
        qp3, kp3, vp3 = (a.reshape(bp, n_prompt, S) for a in (qp, kp, vp))
        qs3, ks3, vs3 = (a.reshape(bs, n_new, S) for a in (qs, ks, vs))

        if kind == 0:
            g = diff_subln_g[j].reshape(1, DIFF_VDIM)
            op = _diff_attention(qp3, kp3, vp3, bias_p, diff_lambda[j], g, i, tq_p, tq_p)
            os_ = _diff_attention(qs3, cache_self_k, cache_self_v, bias_sc, diff_lambda[j], g, i, n_new, tk_s,
                                  new=(ks3, vs3, bias_sn))
        else:
            op = _sb_attention(qp3, kp3, vp3, i, tq_p, tq_p)
            os_ = _sb_attention(qs3, cache_self_k, cache_self_v, i, n_new, tk_s, new=(ks3, vs3))

        mop = _mem_attention(mqp.reshape(bp, n_prompt, MEM_WIDTH), mkp.reshape(bp, n_mem, MEM_WIDTH),
                             mvp.reshape(bp, n_mem, MEM_WIDTH), tq_p)
        mos = _mem_attention(mqs.reshape(bs, n_new, MEM_WIDTH), cmk, cmv, n_new, layer=i)

        outs = []
        for h_res, o_self, o_mem in ((hp, op, mop), (hs, os_, mos)):
            T = h_res.shape[0]
            y, yt = _out_proj(o_self.reshape(T, S), o_mem.reshape(T, MEM_WIDTH), w_o_b, i, h_res,
                              ln_g4[:, 0], ln_b4[:, 0], alpha, ROW_TILE)
            (pq,) = _linear(y, w_pq_b, i, ((0, PEER_HEADS * 2 * HEAD_DIM),), (BF16,), ROW_TILE)
            sel = _peer_select(pq, keys_b, i, ROW_TILE)
            outs.append(_peer_mix(yt, u_b, vt_b, sel, i, y, ln_g4[:, 1], ln_b4[:, 1], alpha, ROW_TILE, PEER_ROWS))
        hp, hs = outs

        new_k_p.append(kp3)
        new_v_p.append(vp3)
        new_mk_p.append(mkp.reshape(bp, n_mem, MEM_HEADS, HEAD_DIM))
        new_mv_p.append(mvp.reshape(bp, n_mem, MEM_HEADS, HEAD_DIM))
        new_k_s.append(ks3)
        new_v_s.append(vs3)

    return (hp.reshape(bp, n_prompt, d_model), hs.reshape(bs, n_new, d_model),
            jnp.stack(new_k_p), jnp.stack(new_v_p), jnp.stack(new_mk_p), jnp.stack(new_mv_p),
            jnp.stack(new_k_s), jnp.stack(new_v_s))
```

```python
import functools
import math

import jax
import jax.numpy as jnp
from jax import lax
from jax.experimental import pallas as pl
from jax.experimental.pallas import tpu as pltpu

F32 = jnp.float32
BF16 = jnp.bfloat16

HEAD_DIM = 128
SELF_WIDTH = 1536
DIFF_HEADS = 6
DIFF_VDIM = 2 * HEAD_DIM
SB_HEADS = 12
MEM_HEADS = 4
MEM_WIDTH = MEM_HEADS * HEAD_DIM
CHUNK = 64
N_BUCKETS = 32
PEER_HEADS = 8
N_KEYS = 128
PEER_TOPK = 16
LN_EPS = 1e-5
SCALE = HEAD_DIM ** -0.5
NEG = -1e30

V7X_VMEM_LIMIT = 56 * 1024 * 1024
LANES = 128
MXU_TILE = 256
ROW_TILE = 512
CACHE_TILE = 2048
SB_PAIR = 2
PEER_MM_ROWS = 512
PEER_ROWS = 4
GATE_ACC_VREGS = 16


def _params(*sem):
    return pltpu.CompilerParams(dimension_semantics=sem, vmem_limit_bytes=V7X_VMEM_LIMIT)


def _tile(n, t):
    t = min(n, t)
    assert n % t == 0, (n, t)
    return t


def _dot(a, b):
    return jnp.dot(a, b, preferred_element_type=F32)


def _dot_nt(a, b):
    return lax.dot_general(a, b, (((1,), (1,)), ((), ())), preferred_element_type=F32)


def _layer_norm(r, g, b):
    mu = jnp.mean(r, axis=-1, keepdims=True)
    d = r - mu
    var = jnp.mean(d * d, axis=-1, keepdims=True)
    return d * lax.rsqrt(var + LN_EPS) * g + b


def _linear_kernel(x_ref, w_ref, *o_refs, segs, nc):
    x = x_ref[...].astype(BF16)
    for o_ref, (start, width) in zip(o_refs, segs):
        for c in range(0, width, nc):
            o_ref[:, c:c + nc] = _dot(x, w_ref[:, start + c:start + c + nc]).astype(o_ref.dtype)


def _linear(x, w, layer, segs, dtypes, tm):
    M, K = x.shape
    N = w.shape[2]
    tm = _tile(M, tm)
    nc = 512
    assert all(wd % nc == 0 for _, wd in segs)
    return pl.pallas_call(
        functools.partial(_linear_kernel, segs=segs, nc=nc),
        grid=(M // tm,),
        in_specs=[
            pl.BlockSpec((tm, K), lambda i: (i, 0)),
            pl.BlockSpec((None, K, N), lambda i: (layer, 0, 0), pipeline_mode=pl.Buffered(1)),
        ],
        out_specs=[pl.BlockSpec((tm, wd), lambda i: (i, 0)) for _, wd in segs],
        out_shape=[jax.ShapeDtypeStruct((M, wd), dt) for (_, wd), dt in zip(segs, dtypes)],
        compiler_params=_params("parallel"),
        name="linear",
    )(x, w)


def _out_proj_kernel(o_ref, mo_ref, w_ref, h_ref, g_ref, b_ref, y_ref, yt_ref, *, alpha):
    acc = _dot(o_ref[...], w_ref[:SELF_WIDTH, :]) + _dot(mo_ref[...], w_ref[SELF_WIDTH:, :])
    y = _layer_norm(alpha * h_ref[...] + acc, g_ref[...], b_ref[...])
    y_ref[...] = y
    yt_ref[...] = y.T.astype(BF16)


def _out_proj(o, mo, w_o, layer, h, ln_g, ln_b, alpha, tm):
    T, D = h.shape
    tm = _tile(T, tm)
    return pl.pallas_call(
        functools.partial(_out_proj_kernel, alpha=alpha),
        grid=(T // tm,),
        in_specs=[
            pl.BlockSpec((tm, SELF_WIDTH), lambda i: (i, 0)),
            pl.BlockSpec((tm, MEM_WIDTH), lambda i: (i, 0)),
            pl.BlockSpec((None, SELF_WIDTH + MEM_WIDTH, D), lambda i: (layer, 0, 0), pipeline_mode=pl.Buffered(1)),
            pl.BlockSpec((tm, D), lambda i: (i, 0)),
            pl.BlockSpec((None, 1, D), lambda i: (layer, 0, 0)),
            pl.BlockSpec((None, 1, D), lambda i: (layer, 0, 0)),
        ],
        out_specs=[pl.BlockSpec((tm, D), lambda i: (i, 0)), pl.BlockSpec((D, tm), lambda i: (0, i))],
        out_shape=[jax.ShapeDtypeStruct((T, D), F32), jax.ShapeDtypeStruct((D, T), BF16)],
        compiler_params=_params("parallel"),
        name="out_proj_ln",
    )(o, mo, w_o, h, ln_g, ln_b)


def _bias_kernel(q0_ref, k0_ref, tab_ref, o_ref, *, tq, tk):
    h = pl.program_id(0)
    t = pl.program_id(1)
    q_pos = q0_ref[t] + lax.broadcasted_iota(jnp.int32, (tq, tk), 0)
    k_pos = k0_ref[t] + lax.broadcasted_iota(jnp.int32, (tq, tk), 1)
    rel = k_pos - q_pos
    n = jnp.abs(rel)
    large = jnp.full((tq, tk), 8, jnp.int32)
    for thr in (12, 16, 23, 32, 46, 64, 91):
        large = large + (n >= thr).astype(jnp.int32)
    bucket = jnp.where(rel > 0, N_BUCKETS // 2, 0) + jnp.where(n < 8, n, large)
    bias = jnp.zeros((tq, tk), F32)
    for b in range(N_BUCKETS):
        bias = jnp.where(bucket == b, tab_ref[b, h], bias)
    visible = (k_pos // CHUNK) <= (q_pos // CHUNK)
    o_ref[...] = jnp.where(visible, bias, NEG)


def _bias_tiles(rel_table, q0, k0, tq, tk):
    n = q0.shape[0]
    return pl.pallas_call(
        functools.partial(_bias_kernel, tq=tq, tk=tk),
        grid_spec=pltpu.PrefetchScalarGridSpec(
            num_scalar_prefetch=2,
            grid=(DIFF_HEADS, n),
            in_specs=[pl.BlockSpec(memory_space=pltpu.SMEM)],
            out_specs=pl.BlockSpec((None, None, tq, tk), lambda h, t, *_: (h, t, 0, 0)),
        ),
        out_shape=jax.ShapeDtypeStruct((DIFF_HEADS, n, tq, tk), F32),
        compiler_params=_params("parallel", "parallel"),
        name="bias_tiles",
    )(q0, k0, rel_table)


def _diff_step(q_ref, k_ref, v_ref, bias_ref, m_sc, l_sc, acc_sc):
    tq, tk = bias_ref.shape
    cw = min(tk, LANES)
    kb = k_ref[...].astype(BF16)
    v_ext = jnp.concatenate([v_ref[...].astype(BF16), jnp.ones((tk, LANES), BF16)], axis=1)
    live = [dict(), dict()]

    def scores(c):
        cs = slice(c * HEAD_DIM, (c + 1) * HEAD_DIM)
        live[c]["s"] = _dot_nt(q_ref[:, cs], kb[:, cs]) * SCALE + bias_ref[...]

    def probs(c):
        s = live[c].pop("s")
        m_prev = m_sc[c]
        m_new = jnp.maximum(m_prev, jnp.broadcast_to(jnp.max(s, axis=1, keepdims=True), (tq, LANES)))
        live[c]["alpha"] = jnp.exp(m_prev - m_new)
        m_sc[c] = m_new
        p = [jnp.exp(s[:, j:j + cw] - m_new[:, :cw]) for j in range(0, tk, cw)]
        live[c]["p"] = jnp.concatenate(p, axis=1).astype(BF16)

    def values(c):
        live[c]["pv"] = _dot(live[c].pop("p"), v_ext)

    def update(c):
        alpha = live[c].pop("alpha")
        pv = live[c].pop("pv")
        acc_sc[c] = jnp.concatenate([alpha, alpha], axis=1) * acc_sc[c] + pv[:, :DIFF_VDIM]
        l_sc[c] = alpha * l_sc[c] + pv[:, DIFF_VDIM:]

    stages = (scores, probs, values, update)
    for t in range(2 + len(stages) - 1):
        for s in reversed(range(len(stages))):
            if 0 <= t - s < 2:
                stages[s](t - s)


def _diff_kernel(*refs, has_new, lam_init):
    if has_new:
        lam_ref, g_ref, q_ref, k_ref, v_ref, bias_ref, kn_ref, vn_ref, biasn_ref, o_ref, m_sc, l_sc, acc_sc = refs
    else:
        lam_ref, g_ref, q_ref, k_ref, v_ref, bias_ref, o_ref, m_sc, l_sc, acc_sc = refs
    qi = pl.program_id(2)
    kk = pl.program_id(3)

    @pl.when(kk == 0)
    def _():
        m_sc[...] = jnp.full(m_sc.shape, NEG, F32)
        l_sc[...] = jnp.zeros(l_sc.shape, F32)
        acc_sc[...] = jnp.zeros(acc_sc.shape, F32)

    if has_new:
        @pl.when(kk == 0)
        def _():
            _diff_step(q_ref, kn_ref, vn_ref, biasn_ref, m_sc, l_sc, acc_sc)

        @pl.when(kk > 0)
        def _():
            _diff_step(q_ref, k_ref, v_ref, bias_ref, m_sc, l_sc, acc_sc)
    else:
        @pl.when(kk <= qi)
        def _():
            _diff_step(q_ref, k_ref, v_ref, bias_ref, m_sc, l_sc, acc_sc)

    @pl.when(kk == pl.num_programs(3) - 1)
    def _():
        lp = lam_ref[...]
        lam = (jnp.exp(jnp.sum(lp[0:1] * lp[1:2], axis=1, keepdims=True))
               - jnp.exp(jnp.sum(lp[2:3] * lp[3:4], axis=1, keepdims=True)) + lam_init)
        l0 = jnp.concatenate([l_sc[0], l_sc[0]], axis=1)
        l1 = jnp.concatenate([l_sc[1], l_sc[1]], axis=1)
        o = acc_sc[0] / l0 - lam * (acc_sc[1] / l1)
        o = o * lax.rsqrt(jnp.mean(o * o, axis=1, keepdims=True) + LN_EPS) * g_ref[...]
        o_ref[...] = (o * (1.0 - lam_init)).astype(o_ref.dtype)


def _diff_attention(q, k, v, bias, lam_vec, subln_g, layer, tq, tk, new=None):
    B, Tq, _ = q.shape
    lam_init = 0.8 - 0.6 * math.exp(-0.3 * layer)
    has_new = new is not None
    if has_new:
        nkc = k.shape[2] // tk
        nk = nkc + 1
        kv_spec = pl.BlockSpec((None, None, tk, DIFF_VDIM),
                               lambda b, h, qi, kk: (layer, b, jnp.clip(nkc - kk, 0, nkc - 1), h))
        bias_spec = pl.BlockSpec((None, None, tq, tk), lambda b, h, qi, kk: (h, jnp.clip(nkc - kk, 0, nkc - 1), 0, 0))
        tn = new[0].shape[1]
        new_specs = [pl.BlockSpec((None, tn, DIFF_VDIM), lambda b, h, qi, kk: (b, 0, h)),
                     pl.BlockSpec((None, tn, DIFF_VDIM), lambda b, h, qi, kk: (b, 0, h)),
                     pl.BlockSpec((None, None, tq, tn), lambda b, h, qi, kk: (h, 0, 0, 0))]
        new_args = list(new)
    else:
        nk = k.shape[1] // tk
        kv_spec = pl.BlockSpec((None, tk, DIFF_VDIM), lambda b, h, qi, kk: (b, jnp.maximum(qi - kk, 0), h))
        bias_spec = pl.BlockSpec((None, None, tq, tk), lambda b, h, qi, kk: (h, jnp.minimum(kk, qi), 0, 0))
        new_specs, new_args = [], []
    return pl.pallas_call(
        functools.partial(_diff_kernel, has_new=has_new, lam_init=lam_init),
        grid=(B, DIFF_HEADS, Tq // tq, nk),
        in_specs=[
            pl.BlockSpec((4, HEAD_DIM), lambda b, h, qi, kk: (0, 0)),
            pl.BlockSpec((1, DIFF_VDIM), lambda b, h, qi, kk: (0, 0)),
            pl.BlockSpec((None, tq, DIFF_VDIM), lambda b, h, qi, kk: (b, qi, h)),
            kv_spec, kv_spec, bias_spec, *new_specs,
        ],
        out_specs=pl.BlockSpec((None, tq, DIFF_VDIM), lambda b, h, qi, kk: (b, qi, h)),
        out_shape=jax.ShapeDtypeStruct((B, Tq, SELF_WIDTH), BF16),
        scratch_shapes=[pltpu.VMEM((2, tq, LANES), F32), pltpu.VMEM((2, tq, LANES), F32),
                        pltpu.VMEM((2, tq, DIFF_VDIM), F32)],
        compiler_params=_params("parallel", "parallel", "parallel", "arbitrary"),
        name="diff_attention",
    )(lam_vec, subln_g, q, k, v, bias, *new_args)


def _sb_step(q_ref, k_ref, v_ref, q_pos0, k_pos0, acc_sc, r_sc, cw, masked):
    tq = q_ref.shape[0]
    tk = k_ref.shape[0]
    cw = min(cw, tk)
    later = (lax.broadcasted_iota(jnp.int32, (cw, cw), 0) > lax.broadcasted_iota(jnp.int32, (cw, cw), 1))
    later = jnp.where(later, 1.0, 0.0).astype(BF16)
    units = [(c0, hh) for c0 in range(tk - cw, -1, -cw) for hh in range(SB_PAIR)]
    live = [dict() for _ in units]

    def scores(u):
        c0, hh = units[u]
        hs = slice(hh * HEAD_DIM, (hh + 1) * HEAD_DIM)
        live[u]["z"] = _dot_nt(q_ref[:, hs], k_ref[c0:c0 + cw, hs].astype(BF16)) * SCALE

    def logs(u):
        c0, _ = units[u]
        z = live[u].pop("z")
        log_keep = -(jnp.maximum(z, 0.0) + jnp.log(1.0 + jnp.exp(-jnp.abs(z))))
        live[u]["log_beta"] = log_keep + z
        if masked:
            q_pos = q_pos0 + lax.broadcasted_iota(jnp.int32, (tq, cw), 0)
            before = (k_pos0 + c0 + lax.broadcasted_iota(jnp.int32, (tq, cw), 1)) < q_pos
            log_keep = jnp.where(before, log_keep, 0.0)
            live[u]["before"] = before
        live[u]["log_keep"] = log_keep

    def cumsum(u):
        live[u]["cum"] = _dot(live[u]["log_keep"].astype(BF16), later)

    def weights(u):
        _, hh = units[u]
        a = jnp.exp(live[u].pop("log_beta") + (live[u].pop("cum") + r_sc[hh]))
        if masked:
            a = jnp.where(live[u].pop("before"), a, 0.0)
        live[u]["a"] = a.astype(BF16)
        r_sc[hh] += jnp.sum(live[u].pop("log_keep"), axis=1, keepdims=True)

    def values(u):
        c0, hh = units[u]
        hs = slice(hh * HEAD_DIM, (hh + 1) * HEAD_DIM)
        acc_sc[hh] += _dot(live[u].pop("a"), v_ref[c0:c0 + cw, hs].astype(BF16))

    stages = (scores, logs, cumsum, weights, values)
    for t in range(len(units) + len(stages) - 1):
        for s in reversed(range(len(stages))):
            if 0 <= t - s < len(units):
                stages[s](t - s)


def _sb_kernel(*refs, has_new, tq, tk, q_off, nkc, cw):
    if has_new:
        q_ref, k_ref, v_ref, kn_ref, vn_ref, o_ref, acc_sc, r_sc = refs
    else:
        q_ref, k_ref, v_ref, o_ref, acc_sc, r_sc = refs
    qi = pl.program_id(2)
    kk = pl.program_id(3)

    @pl.when(kk == 0)
    def _():
        acc_sc[...] = jnp.zeros(acc_sc.shape, F32)
        r_sc[...] = jnp.zeros(r_sc.shape, F32)

    if has_new:
        @pl.when(kk == 0)
        def _():
            _sb_step(q_ref, kn_ref, vn_ref, q_off, q_off, acc_sc, r_sc, cw, True)

        @pl.when(kk > 0)
        def _():
            _sb_step(q_ref, k_ref, v_ref, q_off, (nkc - kk) * tk, acc_sc, r_sc, cw, False)
    else:
        @pl.when(kk == 0)
        def _():
            _sb_step(q_ref, k_ref, v_ref, qi * tq, qi * tk, acc_sc, r_sc, cw, True)

        @pl.when((kk > 0) & (kk <= qi))
        def _():
            _sb_step(q_ref, k_ref, v_ref, qi * tq, (qi - kk) * tk, acc_sc, r_sc, cw, False)

    @pl.when(kk == pl.num_programs(3) - 1)
    def _():
        for hh in range(SB_PAIR):
            o_ref[:, hh * HEAD_DIM:(hh + 1) * HEAD_DIM] = acc_sc[hh].astype(o_ref.dtype)


def _sb_attention(q, k, v, layer, tq, tk, new=None):
    B, Tq, _ = q.shape
    has_new = new is not None
    hw = SB_PAIR * HEAD_DIM
    if has_new:
        nkc = k.shape[2] // tk
        nk = nkc + 1
        q_off = k.shape[2]
        kv_spec = pl.BlockSpec((None, None, tk, hw),
                               lambda b, h, qi, kk: (layer, b, jnp.clip(nkc - kk, 0, nkc - 1), h))
        tn = new[0].shape[1]
        new_specs = [pl.BlockSpec((None, tn, hw), lambda b, h, qi, kk: (b, 0, h))] * 2
        new_args = list(new)
    else:
        assert tq == tk
        nkc = k.shape[1] // tk
        nk = nkc
        q_off = 0
        kv_spec = pl.BlockSpec((None, tk, hw), lambda b, h, qi, kk: (b, jnp.maximum(qi - kk, 0), h))
        new_specs, new_args = [], []
    return pl.pallas_call(
        functools.partial(_sb_kernel, has_new=has_new, tq=tq, tk=tk, q_off=q_off, nkc=nkc, cw=MXU_TILE),
        grid=(B, SB_HEADS // SB_PAIR, Tq // tq, nk),
        in_specs=[pl.BlockSpec((None, tq, hw), lambda b, h, qi, kk: (b, qi, h)), kv_spec, kv_spec, *new_specs],
        out_specs=pl.BlockSpec((None, tq, hw), lambda b, h, qi, kk: (b, qi, h)),
        out_shape=jax.ShapeDtypeStruct((B, Tq, SELF_WIDTH), BF16),
        scratch_shapes=[pltpu.VMEM((SB_PAIR, tq, HEAD_DIM), F32), pltpu.VMEM((SB_PAIR, tq, 1), F32)],
        compiler_params=_params("parallel", "parallel", "parallel", "arbitrary"),
        name="sb_attention",
    )(q, k, v, *new_args)


def _mem_kernel(q_ref, k_ref, v_ref, o_ref):
    q = q_ref[...]
    kb = k_ref[...].astype(BF16)
    vb = v_ref[...].astype(BF16)
    for h in range(MEM_HEADS):
        sl = slice(h * HEAD_DIM, (h + 1) * HEAD_DIM)
        s = _dot_nt(q[:, sl], kb[:, sl]) * SCALE
        p = jnp.exp(s - jnp.max(s, axis=1, keepdims=True))
        o = _dot(p.astype(BF16), vb[:, sl]) / jnp.sum(p, axis=1, keepdims=True)
        o_ref[:, sl] = o.astype(o_ref.dtype)


def _mem_attention(mq, mk, mv, tq, layer=None):
    B, Tq, _ = mq.shape
    n_mem = mk.shape[-2]
    if layer is None:
        kv_spec = pl.BlockSpec((None, n_mem, MEM_WIDTH), lambda b, qi: (b, 0, 0))
    else:
        kv_spec = pl.BlockSpec((None, None, n_mem, MEM_WIDTH), lambda b, qi: (layer, b, 0, 0))
    return pl.pallas_call(
        _mem_kernel,
        grid=(B, Tq // tq),
        in_specs=[pl.BlockSpec((None, tq, MEM_WIDTH), lambda b, qi: (b, qi, 0)), kv_spec, kv_spec],
        out_specs=pl.BlockSpec((None, tq, MEM_WIDTH), lambda b, qi: (b, qi, 0)),
        out_shape=jax.ShapeDtypeStruct((B, Tq, MEM_WIDTH), BF16),
        compiler_params=_params("parallel", "parallel"),
        name="mem_attention",
    )(mq, mk, mv)


def _top_values(x, n):
    out = []
    for _ in range(n):
        m = jnp.max(x, axis=0, keepdims=True)
        out.append(m)
        x = jnp.where(x == m, -jnp.inf, x)
    return out


def _batcher_pairs(n):
    pairs = []
    p = 1
    while p < n:
        k = p
        while k >= 1:
            for j in range(k % p, n - k, 2 * k):
                for i in range(min(k, n - j - k)):
                    if (i + j) // (2 * p) == (i + j + k) // (2 * p):
                        pairs.append((i + j, i + j + k))
            k //= 2
        p *= 2
    return pairs


def _top_values_sorted(x, n):
    g = x.shape[0] // 8
    cols = []
    for lg in range(x.shape[1] // LANES):
        v = [x[r * 8:(r + 1) * 8, lg * LANES:(lg + 1) * LANES] for r in range(g)]
        for i, j in _batcher_pairs(g):
            v[i], v[j] = jnp.maximum(v[i], v[j]), jnp.minimum(v[i], v[j])
        v.append(jnp.full_like(v[0], -jnp.inf))
        out = []
        for k in range(n):
            m = jnp.max(v[0], axis=0, keepdims=True)
            out.append(m)
            popped = v[0] == m
            for d in range(min(n - 1 - k, g)):
                v[d] = jnp.where(popped, v[d + 1], v[d])
        cols.append(out)
    return [jnp.concatenate([c[k] for c in cols], axis=1) for k in range(n)]


def _peer_select_kernel(q_ref, keys_ref, thr_ref, e1z_ref, s2_ref, e2_ref, cand_sc):
    n_top = PEER_TOPK + 1
    pairs = [(p, q) for p in range(n_top) for q in range(n_top) if (p + 1) * (q + 1) <= n_top]
    for h in range(PEER_HEADS):
        s = []
        for c in range(2):
            col = (2 * h + c) * HEAD_DIM
            s.append(_dot_nt(keys_ref[h, c], q_ref[:, col:col + HEAD_DIM]))
        a = _top_values_sorted(s[0], n_top)
        b = _top_values_sorted(s[1], n_top)
        cand_sc[...] = jnp.full(cand_sc.shape, -jnp.inf, F32)
        for r, (p, q) in enumerate(pairs):
            cand_sc[r:r + 1, :] = a[p] + b[q]
        c = _top_values_sorted(cand_sc[...], n_top)
        z = jnp.ones_like(c[0])
        for kth in range(1, PEER_TOPK):
            z = z + jnp.exp(c[kth] - c[0])
        tau = 0.5 * (c[PEER_TOPK - 1] + c[PEER_TOPK])
        thr_ref[h] = tau - s[0]
        e1z_ref[h] = jnp.exp(s[0] - a[0]) / z
        e2 = jnp.exp(s[1] - b[0])
        for lg in range(s2_ref.shape[1]):
            s2_ref[h, lg] = s[1][:, lg * LANES:(lg + 1) * LANES]
            e2_ref[h, lg] = e2[:, lg * LANES:(lg + 1) * LANES]


def _peer_select(qp, keys, layer, tt):
    T = qp.shape[0]
    tt = _tile(T, tt)
    n_pairs = sum(1 for p in range(1, PEER_TOPK + 2) for q in range(1, PEER_TOPK + 2) if p * q <= PEER_TOPK + 1)
    out = jax.ShapeDtypeStruct((PEER_HEADS, N_KEYS, T), F32)
    spec = pl.BlockSpec((PEER_HEADS, N_KEYS, tt), lambda i: (0, 0, i))
    out_lg = jax.ShapeDtypeStruct((PEER_HEADS, T // LANES, N_KEYS, LANES), F32)
    spec_lg = pl.BlockSpec((PEER_HEADS, tt // LANES, N_KEYS, LANES), lambda i: (0, i, 0, 0))
    return pl.pallas_call(
        _peer_select_kernel,
        grid=(T // tt,),
        in_specs=[pl.BlockSpec((tt, PEER_HEADS * 2 * HEAD_DIM), lambda i: (i, 0)),
                  pl.BlockSpec((None, PEER_HEADS, 2, N_KEYS, HEAD_DIM), lambda i: (layer, 0, 0, 0, 0))],
        out_specs=[spec, spec, spec_lg, spec_lg],
        out_shape=[out, out, out_lg, out_lg],
        scratch_shapes=[pltpu.VMEM((8 * pl.next_power_of_2(-(-n_pairs // 8)), tt), F32)],
        compiler_params=_params("parallel"),
        name="peer_select",
    )(qp, keys)


def _peer_gate(thr_ref, e1z_ref, s2_ref, e2_ref, act_ref, hid_ref, rows, row0, slab, lg, sb):
    ls = slice(lg * LANES, (lg + 1) * LANES)
    js = slice(sb * slab, (sb + 1) * slab)
    gates = [None] * rows
    for h in range(PEER_HEADS):
        s2 = s2_ref[h, lg, js, :]
        e2 = e2_ref[h, lg, js, :]
        for ii in range(rows):
            thr = jnp.broadcast_to(thr_ref[h, row0 + ii:row0 + ii + 1, ls], (slab, LANES))
            e1z = jnp.broadcast_to(e1z_ref[h, row0 + ii:row0 + ii + 1, ls], (slab, LANES))
            g = jnp.where(s2 >= thr, e2 * e1z, 0.0)
            gates[ii] = g if gates[ii] is None else gates[ii] + g
    for ii in range(rows):
        es = slice(ii * N_KEYS + sb * slab, ii * N_KEYS + (sb + 1) * slab)
        a = act_ref[es, ls]
        hid_ref[es, ls] = (0.5 * a * (1.0 + lax.erf(a * (2.0 ** -0.5))) * gates[ii]).astype(BF16)


def _peer_mix_kernel(xt_ref, u_ref, vt_ref, thr_ref, e1z_ref, s2_ref, e2_ref, h_ref, g_ref, b_ref,
                     y_ref, acc_sc, act0, act1, hid0, hid1, *, rows, ne, alpha):
    f = pl.program_id(0)
    tt = xt_ref.shape[1]
    slab = GATE_ACC_VREGS * 8 // rows

    @pl.when(f == 0)
    def _():
        acc_sc[...] = jnp.zeros(acc_sc.shape, F32)
        for r in (act0, act1, hid0, hid1):
            r[...] = jnp.zeros(r.shape, r.dtype)

    def stages(act_w, act_r, hid_w, hid_r, row0):
        mt = MXU_TILE
        pieces = []
        kh = xt_ref.shape[0] // 2
        m1 = min(act_w.shape[0], PEER_MM_ROWS)
        for n in range(tt // mt):
            for m in range(act_w.shape[0] // m1):
                for k in range(2):
                    def mm1(m=m, n=n, k=k):
                        d = _dot(u_ref[m * m1:(m + 1) * m1, k * kh:(k + 1) * kh],
                                 xt_ref[k * kh:(k + 1) * kh, n * mt:(n + 1) * mt])
                        if k == 0:
                            act_w[m * m1:(m + 1) * m1, n * mt:(n + 1) * mt] = d
                        else:
                            act_w[m * m1:(m + 1) * m1, n * mt:(n + 1) * mt] += d
                    pieces.append((mm1, (kh // mt) * (m1 // mt)))
        m2 = min(acc_sc.shape[0], PEER_MM_ROWS)
        for m in range(acc_sc.shape[0] // m2):
            for n in range(tt // mt):
                def mm2(m=m, n=n):
                    acc_sc[m * m2:(m + 1) * m2, n * mt:(n + 1) * mt] += _dot(
                        vt_ref[m * m2:(m + 1) * m2, :], hid_r[:, n * mt:(n + 1) * mt])
                pieces.append((mm2, (hid_r.shape[0] // mt) * (m2 // mt)))
        groups = [(lg, sb) for lg in range(tt // LANES) for sb in range(N_KEYS // slab)]
        total = sum(c for _, c in pieces)
        issued = 0
        for gi, (lg, sb) in enumerate(groups):
            while pieces and issued * len(groups) <= gi * total:
                fn, cost = pieces.pop(0)
                fn()
                issued += cost
            _peer_gate(thr_ref, e1z_ref, s2_ref, e2_ref, act_r, hid_w, rows, row0, slab, lg, sb)
        for fn, _ in pieces:
            fn()

    @pl.when(f % 2 == 0)
    def _():
        stages(act0, act1, hid1, hid0, rows % 8)

    @pl.when(f % 2 == 1)
    def _():
        stages(act1, act0, hid0, hid1, 0)

    @pl.when((f >= 2) & ((f - 2) % ne == ne - 1))
    def _():
        y_ref[...] = _layer_norm(alpha * h_ref[...] + acc_sc[...].T, g_ref[...], b_ref[...])
        acc_sc[...] = jnp.zeros(acc_sc.shape, F32)


def _peer_mix(xt, u, vt, sel, layer, h, ln_g, ln_b, alpha, tt, rows):
    D, T = xt.shape
    tt = _tile(T, tt)
    ne, eb = vt.shape[1], vt.shape[3]
    assert eb == rows * N_KEYS and ne * eb == u.shape[1] and rows in (4, 8) and ne % 2 == 0
    last = (T // tt) * ne - 1

    def tile_of(f):
        return jnp.clip(f, 0, last) // ne

    def block_of(f):
        return jnp.clip(f, 0, last) % ne

    row_spec = pl.BlockSpec((PEER_HEADS, None, 8, tt), lambda f: (0, block_of(f - 1) * rows // 8, 0, tile_of(f - 1)))
    full_spec = pl.BlockSpec((PEER_HEADS, tt // LANES, N_KEYS, LANES), lambda f: (0, tile_of(f - 1), 0, 0))
    thr, e1z, s2, e2 = sel
    thr = thr.reshape(PEER_HEADS, N_KEYS // 8, 8, T)
    e1z = e1z.reshape(PEER_HEADS, N_KEYS // 8, 8, T)
    return pl.pallas_call(
        functools.partial(_peer_mix_kernel, rows=rows, ne=ne, alpha=alpha),
        grid=(last + 3,),
        in_specs=[
            pl.BlockSpec((D, tt), lambda f: (0, tile_of(f))),
            pl.BlockSpec((None, eb, D), lambda f: (layer, block_of(f), 0)),
            pl.BlockSpec((None, None, D, eb), lambda f: (layer, block_of(f - 2), 0, 0)),
            row_spec, row_spec, full_spec, full_spec,
            pl.BlockSpec((tt, D), lambda f: (tile_of(f - 2), 0), pipeline_mode=pl.Buffered(1)),
            pl.BlockSpec((None, 1, D), lambda f: (layer, 0, 0)),
            pl.BlockSpec((None, 1, D), lambda f: (layer, 0, 0)),
        ],
        out_specs=pl.BlockSpec((tt, D), lambda f: (tile_of(f - 2), 0)),
        out_shape=jax.ShapeDtypeStruct((T, D), F32),
        scratch_shapes=[pltpu.VMEM((D, tt), F32), pltpu.VMEM((eb, tt), F32), pltpu.VMEM((eb, tt), F32),
                        pltpu.VMEM((eb, tt), BF16), pltpu.VMEM((eb, tt), BF16)],
        compiler_params=_params("arbitrary"),
        name="peer_mix",
    )(xt, u, vt, thr, e1z, s2, e2, h, ln_g, ln_b)


def kernel(x_prompt, x_sample, cache_self_k, cache_self_v, cache_mem_k, cache_mem_v, mem_prompt, w_in, w_o, w_mem_k, w_mem_v, rel_bias_table, diff_lambda, diff_subln_g, ln_g, ln_b, peer_w_q, peer_sub_keys, peer_u, peer_v):
    depth = w_in.shape[0]
    bp, n_prompt, d_model = x_prompt.shape
    bs, n_new, _ = x_sample.shape
    n_past = cache_self_k.shape[2]
    n_mem = mem_prompt.shape[1]
    alpha = (2 * depth) ** 0.25
    S = SELF_WIDTH

    w_in_b = w_in.astype(BF16)
    w_o_b = w_o.astype(BF16)
    w_mk_b = w_mem_k.astype(BF16)
    w_mv_b = w_mem_v.astype(BF16)
    w_pq_b = peer_w_q.astype(BF16)
    keys_b = peer_sub_keys.astype(BF16)
    u_b = peer_u.astype(BF16)
    eb = PEER_ROWS * N_KEYS
    vt_b = jnp.swapaxes(peer_v.reshape(depth, -1, eb, d_model), 2, 3).astype(BF16)
    ln_g4 = ln_g.reshape(depth, 2, 1, d_model)
    ln_b4 = ln_b.reshape(depth, 2, 1, d_model)
    cmk = cache_mem_k.reshape(depth, bs, n_mem, MEM_WIDTH)
    cmv = cache_mem_v.reshape(depth, bs, n_mem, MEM_WIDTH)
    mem2d = mem_prompt.reshape(bp * n_mem, d_model)

    tq_p = _tile(n_prompt, 512)
    tk_s = _tile(n_past, CACHE_TILE)
    nq_p = n_prompt // tq_p
    nk_s = n_past // tk_s
    i32 = jnp.int32
    bias_p = _bias_tiles(rel_bias_table, jnp.arange(nq_p, dtype=i32) * tq_p, jnp.zeros((nq_p,), i32), tq_p, tq_p)
    bias_sc = _bias_tiles(rel_bias_table, jnp.full((nk_s,), n_past, i32), jnp.arange(nk_s, dtype=i32) * tk_s, n_new, tk_s)
    bias_sn = _bias_tiles(rel_bias_table, jnp.full((1,), n_past, i32), jnp.full((1,), n_past, i32), n_new, n_new)

    hp = x_prompt.reshape(bp * n_prompt, d_model)
    hs = x_sample.reshape(bs * n_new, d_model)
    segs = ((0, S), (S, S), (2 * S, S), (3 * S, MEM_WIDTH))
    seg_dt = (BF16, F32, F32, BF16)
    new_k_p, new_v_p, new_mk_p, new_mv_p, new_k_s, new_v_s = [], [], [], [], [], []
    for i in range(depth):
        kind = i % 2
        j = i // 2
        qp, kp, vp, mqp = _linear(hp, w_in_b, i, segs, seg_dt, ROW_TILE)
        qs, ks, vs, mqs = _linear(hs, w_in_b, i, segs, seg_dt, ROW_TILE)
        (mkp,) = _linear(mem2d, w_mk_b, i, ((0, MEM_WIDTH),), (F32,), ROW_TILE)
        (mvp,) = _linear(mem2d, w_mv_b, i, ((0, MEM_WIDTH),), (F32,), ROW_TILE)
        qp3, kp3, vp3 = (a.reshape(bp, n_prompt, S) for a in (qp, kp, vp))
        qs3, ks3, vs3 = (a.reshape(bs, n_new, S) for a in (qs, ks, vs))

        if kind == 0:
            g = diff_subln_g[j].reshape(1, DIFF_VDIM)
            op = _diff_attention(qp3, kp3, vp3, bias_p, diff_lambda[j], g, i, tq_p, tq_p)
            os_ = _diff_attention(qs3, cache_self_k, cache_self_v, bias_sc, diff_lambda[j], g, i, n_new, tk_s,
                                  new=(ks3, vs3, bias_sn))
        else:
            op = _sb_attention(qp3, kp3, vp3, i, tq_p, tq_p)
            os_ = _sb_attention(qs3, cache_self_k, cache_self_v, i, n_new, tk_s, new=(ks3, vs3))

        mop = _mem_attention(mqp.reshape(bp, n_prompt, MEM_WIDTH), mkp.reshape(bp, n_mem, MEM_WIDTH),
                             mvp.reshape(bp, n_mem, MEM_WIDTH), tq_p)
        mos = _mem_attention(mqs.reshape(bs, n_new, MEM_WIDTH), cmk, cmv, n_new, layer=i)

        outs = []
        for h_res, o_self, o_mem in ((hp, op, mop), (hs, os_, mos)):
            T = h_res.shape[0]
            y, yt = _out_proj(o_self.reshape(T, S), o_mem.reshape(T, MEM_WIDTH), w_o_b, i, h_res,
                              ln_g4[:, 0], ln_b4[:, 0], alpha, ROW_TILE)
            (pq,) = _linear(y, w_pq_b, i, ((0, PEER_HEADS * 2 * HEAD_DIM),), (BF16,), ROW_TILE)
            sel = _peer_select(pq, keys_b, i, ROW_TILE)
            outs.append(_peer_mix(yt, u_b, vt_b, sel, i, y, ln_g4[:, 1], ln_b4[:, 1], alpha, ROW_TILE, PEER_ROWS))
        hp, hs = outs

        new_k_p.append(kp3)
        new_v_p.append(vp3)
        new_mk_p.append(mkp.reshape(bp, n_mem, MEM_HEADS, HEAD_DIM))
        new_mv_p.append(mvp.reshape(bp, n_mem, MEM_HEADS, HEAD_DIM))
        new_k_s.append(ks3)
        new_v_s.append(vs3)

    return (hp.reshape(bp, n_prompt, d_model), hs.reshape(bs, n_new, d_model),
            jnp.stack(new_k_p), jnp.stack(new_v_p), jnp.stack(new_mk_p), jnp.stack(new_mv_p),
            jnp.stack(new_k_s), jnp.stack(new_v_s))
```

```python
import functools
import math

import jax
import jax.numpy as jnp
from jax import lax
from jax.experimental import pallas as pl
from jax.experimental.pallas import tpu as pltpu

F32 = jnp.float32
BF16 = jnp.bfloat16

HEAD_DIM = 128
SELF_WIDTH = 1536
DIFF_HEADS = 6
DIFF_VDIM = 2 * HEAD_DIM
SB_HEADS = 12
MEM_HEADS = 4
MEM_WIDTH = MEM_HEADS * HEAD_DIM
CHUNK = 64
N_BUCKETS = 32
PEER_HEADS = 8
N_KEYS = 128
PEER_TOPK = 16
LN_EPS = 1e-5
SCALE = HEAD_DIM ** -0.5
NEG = -1e30

V7X_VMEM_LIMIT = 56 * 1024 * 1024
LANES = 128
MXU_TILE = 256
ROW_TILE = 512
CACHE_TILE = 2048
SB_PAIR = 2
PEER_MM_ROWS = 512
PEER_ROWS = 4
GATE_ACC_VREGS = 16


def _params(*sem):
    return pltpu.CompilerParams(dimension_semantics=sem, vmem_limit_bytes=V7X_VMEM_LIMIT)


def _tile(n, t):
    t = min(n, t)
    assert n % t == 0, (n, t)
    return t


def _dot(a, b):
    return jnp.dot(a, b, preferred_element_type=F32)


def _dot_nt(a, b):
    return lax.dot_general(a, b, (((1,), (1,)), ((), ())), preferred_element_type=F32)


def _layer_norm(r, g, b):
    mu = jnp.mean(r, axis=-1, keepdims=True)
    d = r - mu
    var = jnp.mean(d * d, axis=-1, keepdims=True)
    return d * lax.rsqrt(var + LN_EPS) * g + b


def _linear_kernel(x_ref, w_ref, *o_refs, segs, nc):
    x = x_ref[...].astype(BF16)
    for o_ref, (start, width) in zip(o_refs, segs):
        for c in range(0, width, nc):
            o_ref[:, c:c + nc] = _dot(x, w_ref[:, start + c:start + c + nc]).astype(o_ref.dtype)


def _linear(x, w, layer, segs, dtypes, tm):
    M, K = x.shape
    N = w.shape[2]
    tm = _tile(M, tm)
    nc = 512
    assert all(wd % nc == 0 for _, wd in segs)
    return pl.pallas_call(
        functools.partial(_linear_kernel, segs=segs, nc=nc),
        grid=(M // tm,),
        in_specs=[
            pl.BlockSpec((tm, K), lambda i: (i, 0)),
            pl.BlockSpec((None, K, N), lambda i: (layer, 0, 0), pipeline_mode=pl.Buffered(1)),
        ],
        out_specs=[pl.BlockSpec((tm, wd), lambda i: (i, 0)) for _, wd in segs],
        out_shape=[jax.ShapeDtypeStruct((M, wd), dt) for (_, wd), dt in zip(segs, dtypes)],
        compiler_params=_params("parallel"),
        name="linear",
    )(x, w)


def _out_proj_kernel(o_ref, mo_ref, w_ref, h_ref, g_ref, b_ref, y_ref, yt_ref, *, alpha):
    acc = _dot(o_ref[...], w_ref[:SELF_WIDTH, :]) + _dot(mo_ref[...], w_ref[SELF_WIDTH:, :])
    y = _layer_norm(alpha * h_ref[...] + acc, g_ref[...], b_ref[...])
    y_ref[...] = y
    yt_ref[...] = y.T.astype(BF16)


def _out_proj(o, mo, w_o, layer, h, ln_g, ln_b, alpha, tm):
    T, D = h.shape
    tm = _tile(T, tm)
    return pl.pallas_call(
        functools.partial(_out_proj_kernel, alpha=alpha),
        grid=(T // tm,),
        in_specs=[
            pl.BlockSpec((tm, SELF_WIDTH), lambda i: (i, 0)),
            pl.BlockSpec((tm, MEM_WIDTH), lambda i: (i, 0)),
            pl.BlockSpec((None, SELF_WIDTH + MEM_WIDTH, D), lambda i: (layer, 0, 0), pipeline_mode=pl.Buffered(1)),
            pl.BlockSpec((tm, D), lambda i: (i, 0)),
            pl.BlockSpec((None, 1, D), lambda i: (layer, 0, 0)),
            pl.BlockSpec((None, 1, D), lambda i: (layer, 0, 0)),
        ],
        out_specs=[pl.BlockSpec((tm, D), lambda i: (i, 0)), pl.BlockSpec((D, tm), lambda i: (0, i))],
        out_shape=[jax.ShapeDtypeStruct((T, D), F32), jax.ShapeDtypeStruct((D, T), BF16)],
        compiler_params=_params("parallel"),
        name="out_proj_ln",
    )(o, mo, w_o, h, ln_g, ln_b)


def _bias_kernel(q0_ref, k0_ref, tab_ref, o_ref, *, tq, tk):
    h = pl.program_id(0)
    t = pl.program_id(1)
    q_pos = q0_ref[t] + lax.broadcasted_iota(jnp.int32, (tq, tk), 0)
    k_pos = k0_ref[t] + lax.broadcasted_iota(jnp.int32, (tq, tk), 1)
    rel = k_pos - q_pos
    n = jnp.abs(rel)
    large = jnp.full((tq, tk), 8, jnp.int32)
    for thr in (12, 16, 23, 32, 46, 64, 91):
        large = large + (n >= thr).astype(jnp.int32)
    bucket = jnp.where(rel > 0, N_BUCKETS // 2, 0) + jnp.where(n < 8, n, large)
    bias = jnp.zeros((tq, tk), F32)
    for b in range(N_BUCKETS):
        bias = jnp.where(bucket == b, tab_ref[b, h], bias)
    visible = (k_pos // CHUNK) <= (q_pos // CHUNK)
    o_ref[...] = jnp.where(visible, bias, NEG)


def _bias_tiles(rel_table, q0, k0, tq, tk):
    n = q0.shape[0]
    return pl.pallas_call(
        functools.partial(_bias_kernel, tq=tq, tk=tk),
        grid_spec=pltpu.PrefetchScalarGridSpec(
            num_scalar_prefetch=2,
            grid=(DIFF_HEADS, n),
            in_specs=[pl.BlockSpec(memory_space=pltpu.SMEM)],
            out_specs=pl.BlockSpec((None, None, tq, tk), lambda h, t, *_: (h, t, 0, 0)),
        ),
        out_shape=jax.ShapeDtypeStruct((DIFF_HEADS, n, tq, tk), F32),
        compiler_params=_params("parallel", "parallel"),
        name="bias_tiles",
    )(q0, k0, rel_table)


def _diff_step(q_ref, k_ref, v_ref, bias_ref, m_sc, l_sc, acc_sc):
    tq, tk = bias_ref.shape
    cw = min(tk, LANES)
    kb = k_ref[...].astype(BF16)
    v_ext = jnp.concatenate([v_ref[...].astype(BF16), jnp.ones((tk, LANES), BF16)], axis=1)
    live = [dict(), dict()]

    def scores(c):
        cs = slice(c * HEAD_DIM, (c + 1) * HEAD_DIM)
        live[c]["s"] = _dot_nt(q_ref[:, cs], kb[:, cs]) * SCALE + bias_ref[...]

    def probs(c):
        s = live[c].pop("s")
        m_prev = m_sc[c]
        m_new = jnp.maximum(m_prev, jnp.broadcast_to(jnp.max(s, axis=1, keepdims=True), (tq, LANES)))
        live[c]["alpha"] = jnp.exp(m_prev - m_new)
        m_sc[c] = m_new
        p = [jnp.exp(s[:, j:j + cw] - m_new[:, :cw]) for j in range(0, tk, cw)]
        live[c]["p"] = jnp.concatenate(p, axis=1).astype(BF16)

    def values(c):
        live[c]["pv"] = _dot(live[c].pop("p"), v_ext)

    def update(c):
        alpha = live[c].pop("alpha")
        pv = live[c].pop("pv")
        acc_sc[c] = jnp.concatenate([alpha, alpha], axis=1) * acc_sc[c] + pv[:, :DIFF_VDIM]
        l_sc[c] = alpha * l_sc[c] + pv[:, DIFF_VDIM:]

    stages = (scores, probs, values, update)
    for t in range(2 + len(stages) - 1):
        for s in reversed(range(len(stages))):
            if 0 <= t - s < 2:
                stages[s](t - s)


def _diff_kernel(qi_ref, kk_ref, fin_ref, *refs, has_new, lam_init):
    if has_new:
        lam_ref, g_ref, q_ref, k_ref, v_ref, bias_ref, kn_ref, vn_ref, biasn_ref, o_ref, m_sc, l_sc, acc_sc = refs
    else:
        lam_ref, g_ref, q_ref, k_ref, v_ref, bias_ref, o_ref, m_sc, l_sc, acc_sc = refs
    t = pl.program_id(2)
    kk = kk_ref[t]

    @pl.when(kk == 0)
    def _():
        m_sc[...] = jnp.full(m_sc.shape, NEG, F32)
        l_sc[...] = jnp.zeros(l_sc.shape, F32)
        acc_sc[...] = jnp.zeros(acc_sc.shape, F32)

    if has_new:
        @pl.when(kk == 0)
        def _():
            _diff_step(q_ref, kn_ref, vn_ref, biasn_ref, m_sc, l_sc, acc_sc)

        @pl.when(kk > 0)
        def _():
            _diff_step(q_ref, k_ref, v_ref, bias_ref, m_sc, l_sc, acc_sc)
    else:
        _diff_step(q_ref, k_ref, v_ref, bias_ref, m_sc, l_sc, acc_sc)

    @pl.when(fin_ref[t] == 1)
    def _():
        lp = lam_ref[...]
        lam = (jnp.exp(jnp.sum(lp[0:1] * lp[1:2], axis=1, keepdims=True))
               - jnp.exp(jnp.sum(lp[2:3] * lp[3:4], axis=1, keepdims=True)) + lam_init)
        l0 = jnp.concatenate([l_sc[0], l_sc[0]], axis=1)
        l1 = jnp.concatenate([l_sc[1], l_sc[1]], axis=1)
        o = acc_sc[0] / l0 - lam * (acc_sc[1] / l1)
        o = o * lax.rsqrt(jnp.mean(o * o, axis=1, keepdims=True) + LN_EPS) * g_ref[...]
        o_ref[...] = (o * (1.0 - lam_init)).astype(o_ref.dtype)


def _attention_steps(nq, nk, causal):
    steps = [(qi, kk) for qi in range(nq) for kk in range(qi + 1 if causal else nk)]
    qi_t = jnp.array([s[0] for s in steps], jnp.int32)
    kk_t = jnp.array([s[1] for s in steps], jnp.int32)
    fin_t = jnp.array([int(kk == (qi if causal else nk - 1)) for qi, kk in steps], jnp.int32)
    return qi_t, kk_t, fin_t


def _diff_attention(q, k, v, bias, lam_vec, subln_g, layer, tq, tk, new=None):
    B, Tq, _ = q.shape
    lam_init = 0.8 - 0.6 * math.exp(-0.3 * layer)
    has_new = new is not None
    if has_new:
        nkc = k.shape[2] // tk
        tables = _attention_steps(Tq // tq, nkc + 1, causal=False)
        kv_spec = pl.BlockSpec((None, None, tk, DIFF_VDIM),
                               lambda b, h, t, qi, kk, fin: (layer, b, jnp.clip(nkc - kk[t], 0, nkc - 1), h))
        bias_spec = pl.BlockSpec((None, None, tq, tk),
                                 lambda b, h, t, qi, kk, fin: (h, jnp.clip(nkc - kk[t], 0, nkc - 1), 0, 0))
        tn = new[0].shape[1]
        new_specs = [pl.BlockSpec((None, tn, DIFF_VDIM), lambda b, h, t, qi, kk, fin: (b, 0, h)),
                     pl.BlockSpec((None, tn, DIFF_VDIM), lambda b, h, t, qi, kk, fin: (b, 0, h)),
                     pl.BlockSpec((None, None, tq, tn), lambda b, h, t, qi, kk, fin: (h, 0, 0, 0))]
        new_args = list(new)
    else:
        tables = _attention_steps(Tq // tq, k.shape[1] // tk, causal=True)
        kv_spec = pl.BlockSpec((None, tk, DIFF_VDIM), lambda b, h, t, qi, kk, fin: (b, qi[t] - kk[t], h))
        bias_spec = pl.BlockSpec((None, None, tq, tk), lambda b, h, t, qi, kk, fin: (h, kk[t], 0, 0))
        new_specs, new_args = [], []
    return pl.pallas_call(
        functools.partial(_diff_kernel, has_new=has_new, lam_init=lam_init),
        grid_spec=pltpu.PrefetchScalarGridSpec(
            num_scalar_prefetch=3,
            grid=(B, DIFF_HEADS, tables[0].shape[0]),
            in_specs=[
                pl.BlockSpec((4, HEAD_DIM), lambda b, h, t, qi, kk, fin: (0, 0)),
                pl.BlockSpec((1, DIFF_VDIM), lambda b, h, t, qi, kk, fin: (0, 0)),
                pl.BlockSpec((None, tq, DIFF_VDIM), lambda b, h, t, qi, kk, fin: (b, qi[t], h)),
                kv_spec, kv_spec, bias_spec, *new_specs,
            ],
            out_specs=pl.BlockSpec((None, tq, DIFF_VDIM), lambda b, h, t, qi, kk, fin: (b, qi[t], h)),
            scratch_shapes=[pltpu.VMEM((2, tq, LANES), F32), pltpu.VMEM((2, tq, LANES), F32),
                            pltpu.VMEM((2, tq, DIFF_VDIM), F32)],
        ),
        out_shape=jax.ShapeDtypeStruct((B, Tq, SELF_WIDTH), BF16),
        compiler_params=_params("parallel", "parallel", "arbitrary"),
        name="diff_attention",
    )(*tables, lam_vec, subln_g, q, k, v, bias, *new_args)


def _sb_step(q_ref, k_ref, v_ref, q_pos0, k_pos0, acc_sc, r_sc, cw, masked):
    tq = q_ref.shape[0]
    tk = k_ref.shape[0]
    cw = min(cw, tk)
    later = (lax.broadcasted_iota(jnp.int32, (cw, cw), 0) > lax.broadcasted_iota(jnp.int32, (cw, cw), 1))
    later = jnp.where(later, 1.0, 0.0).astype(BF16)
    units = [(c0, hh) for c0 in range(tk - cw, -1, -cw) for hh in range(SB_PAIR)]
    live = [dict() for _ in units]

    def scores(u):
        c0, hh = units[u]
        hs = slice(hh * HEAD_DIM, (hh + 1) * HEAD_DIM)
        live[u]["z"] = _dot_nt(q_ref[:, hs], k_ref[c0:c0 + cw, hs].astype(BF16)) * SCALE

    def logs(u):
        c0, _ = units[u]
        z = live[u].pop("z")
        log_keep = -(jnp.maximum(z, 0.0) + jnp.log(1.0 + jnp.exp(-jnp.abs(z))))
        live[u]["log_beta"] = log_keep + z
        if masked:
            q_pos = q_pos0 + lax.broadcasted_iota(jnp.int32, (tq, cw), 0)
            before = (k_pos0 + c0 + lax.broadcasted_iota(jnp.int32, (tq, cw), 1)) < q_pos
            log_keep = jnp.where(before, log_keep, 0.0)
            live[u]["before"] = before
        live[u]["log_keep"] = log_keep

    def cumsum(u):
        live[u]["cum"] = _dot(live[u]["log_keep"].astype(BF16), later)

    def weights(u):
        _, hh = units[u]
        a = jnp.exp(live[u].pop("log_beta") + (live[u].pop("cum") + r_sc[hh]))
        if masked:
            a = jnp.where(live[u].pop("before"), a, 0.0)
        live[u]["a"] = a.astype(BF16)
        r_sc[hh] += jnp.sum(live[u].pop("log_keep"), axis=1, keepdims=True)

    def values(u):
        c0, hh = units[u]
        hs = slice(hh * HEAD_DIM, (hh + 1) * HEAD_DIM)
        acc_sc[hh] += _dot(live[u].pop("a"), v_ref[c0:c0 + cw, hs].astype(BF16))

    stages = (scores, logs, cumsum, weights, values)
    for t in range(len(units) + len(stages) - 1):
        for s in reversed(range(len(stages))):
            if 0 <= t - s < len(units):
                stages[s](t - s)


def _sb_kernel(qi_ref, kk_ref, fin_ref, *refs, has_new, tq, tk, q_off, nkc, cw):
    if has_new:
        q_ref, k_ref, v_ref, kn_ref, vn_ref, o_ref, acc_sc, r_sc = refs
    else:
        q_ref, k_ref, v_ref, o_ref, acc_sc, r_sc = refs
    t = pl.program_id(2)
    qi = qi_ref[t]
    kk = kk_ref[t]

    @pl.when(kk == 0)
    def _():
        acc_sc[...] = jnp.zeros(acc_sc.shape, F32)
        r_sc[...] = jnp.zeros(r_sc.shape, F32)

    if has_new:
        @pl.when(kk == 0)
        def _():
            _sb_step(q_ref, kn_ref, vn_ref, q_off, q_off, acc_sc, r_sc, cw, True)

        @pl.when(kk > 0)
        def _():
            _sb_step(q_ref, k_ref, v_ref, q_off, (nkc - kk) * tk, acc_sc, r_sc, cw, False)
    else:
        @pl.when(kk == 0)
        def _():
            _sb_step(q_ref, k_ref, v_ref, qi * tq, qi * tk, acc_sc, r_sc, cw, True)

        @pl.when(kk > 0)
        def _():
            _sb_step(q_ref, k_ref, v_ref, qi * tq, (qi - kk) * tk, acc_sc, r_sc, cw, False)

    @pl.when(fin_ref[t] == 1)
    def _():
        for hh in range(SB_PAIR):
            o_ref[:, hh * HEAD_DIM:(hh + 1) * HEAD_DIM] = acc_sc[hh].astype(o_ref.dtype)


def _sb_attention(q, k, v, layer, tq, tk, new=None):
    B, Tq, _ = q.shape
    has_new = new is not None
    hw = SB_PAIR * HEAD_DIM
    if has_new:
        nkc = k.shape[2] // tk
        q_off = k.shape[2]
        tables = _attention_steps(Tq // tq, nkc + 1, causal=False)
        kv_spec = pl.BlockSpec((None, None, tk, hw),
                               lambda b, h, t, qi, kk, fin: (layer, b, jnp.clip(nkc - kk[t], 0, nkc - 1), h))
        tn = new[0].shape[1]
        new_specs = [pl.BlockSpec((None, tn, hw), lambda b, h, t, qi, kk, fin: (b, 0, h))] * 2
        new_args = list(new)
    else:
        assert tq == tk
        nkc = k.shape[1] // tk
        q_off = 0
        tables = _attention_steps(Tq // tq, nkc, causal=True)
        kv_spec = pl.BlockSpec((None, tk, hw), lambda b, h, t, qi, kk, fin: (b, qi[t] - kk[t], h))
        new_specs, new_args = [], []
    return pl.pallas_call(
        functools.partial(_sb_kernel, has_new=has_new, tq=tq, tk=tk, q_off=q_off, nkc=nkc, cw=MXU_TILE),
        grid_spec=pltpu.PrefetchScalarGridSpec(
            num_scalar_prefetch=3,
            grid=(B, SB_HEADS // SB_PAIR, tables[0].shape[0]),
            in_specs=[pl.BlockSpec((None, tq, hw), lambda b, h, t, qi, kk, fin: (b, qi[t], h)),
                      kv_spec, kv_spec, *new_specs],
            out_specs=pl.BlockSpec((None, tq, hw), lambda b, h, t, qi, kk, fin: (b, qi[t], h)),
            scratch_shapes=[pltpu.VMEM((SB_PAIR, tq, HEAD_DIM), F32), pltpu.VMEM((SB_PAIR, tq, 1), F32)],
        ),
        out_shape=jax.ShapeDtypeStruct((B, Tq, SELF_WIDTH), BF16),
        compiler_params=_params("parallel", "parallel", "arbitrary"),
        name="sb_attention",
    )(*tables, q, k, v, *new_args)


def _mem_kernel(q_ref, k_ref, v_ref, o_ref):
    q = q_ref[...]
    kb = k_ref[...].astype(BF16)
    vb = v_ref[...].astype(BF16)
    for h in range(MEM_HEADS):
        sl = slice(h * HEAD_DIM, (h + 1) * HEAD_DIM)
        s = _dot_nt(q[:, sl], kb[:, sl]) * SCALE
        p = jnp.exp(s - jnp.max(s, axis=1, keepdims=True))
        o = _dot(p.astype(BF16), vb[:, sl]) / jnp.sum(p, axis=1, keepdims=True)
        o_ref[:, sl] = o.astype(o_ref.dtype)


def _mem_attention(mq, mk, mv, tq, layer=None):
    B, Tq, _ = mq.shape
    n_mem = mk.shape[-2]
    if layer is None:
        kv_spec = pl.BlockSpec((None, n_mem, MEM_WIDTH), lambda b, qi: (b, 0, 0))
    else:
        kv_spec = pl.BlockSpec((None, None, n_mem, MEM_WIDTH), lambda b, qi: (layer, b, 0, 0))
    return pl.pallas_call(
        _mem_kernel,
        grid=(B, Tq // tq),
        in_specs=[pl.BlockSpec((None, tq, MEM_WIDTH), lambda b, qi: (b, qi, 0)), kv_spec, kv_spec],
        out_specs=pl.BlockSpec((None, tq, MEM_WIDTH), lambda b, qi: (b, qi, 0)),
        out_shape=jax.ShapeDtypeStruct((B, Tq, MEM_WIDTH), BF16),
        compiler_params=_params("parallel", "parallel"),
        name="mem_attention",
    )(mq, mk, mv)


def _top_values(x, n):
    out = []
    for _ in range(n):
        m = jnp.max(x, axis=0, keepdims=True)
        out.append(m)
        x = jnp.where(x == m, -jnp.inf, x)
    return out


def _batcher_pairs(n):
    pairs = []
    p = 1
    while p < n:
        k = p
        while k >= 1:
            for j in range(k % p, n - k, 2 * k):
                for i in range(min(k, n - j - k)):
                    if (i + j) // (2 * p) == (i + j + k) // (2 * p):
                        pairs.append((i + j, i + j + k))
            k //= 2
        p *= 2
    return pairs


def _top_values_sorted(x, n):
    g = x.shape[0] // 8
    cols = []
    for lg in range(x.shape[1] // LANES):
        v = [x[r * 8:(r + 1) * 8, lg * LANES:(lg + 1) * LANES] for r in range(g)]
        for i, j in _batcher_pairs(g):
            v[i], v[j] = jnp.maximum(v[i], v[j]), jnp.minimum(v[i], v[j])
        v.append(jnp.full_like(v[0], -jnp.inf))
        out = []
        for k in range(n):
            m = jnp.max(v[0], axis=0, keepdims=True)
            out.append(m)
            popped = v[0] == m
            for d in range(min(n - 1 - k, g)):
                v[d] = jnp.where(popped, v[d + 1], v[d])
        cols.append(out)
    return [jnp.concatenate([c[k] for c in cols], axis=1) for k in range(n)]


def _peer_select_kernel(q_ref, keys_ref, thr_ref, e1z_ref, s2_ref, e2_ref, cand_sc):
    n_top = PEER_TOPK + 1
    pairs = [(p, q) for p in range(n_top) for q in range(n_top) if (p + 1) * (q + 1) <= n_top]
    for h in range(PEER_HEADS):
        s = []
        for c in range(2):
            col = (2 * h + c) * HEAD_DIM
            s.append(_dot_nt(keys_ref[h, c], q_ref[:, col:col + HEAD_DIM]))
        a = _top_values_sorted(s[0], n_top)
        b = _top_values_sorted(s[1], n_top)
        cand_sc[...] = jnp.full(cand_sc.shape, -jnp.inf, F32)
        for r, (p, q) in enumerate(pairs):
            cand_sc[r:r + 1, :] = a[p] + b[q]
        c = _top_values_sorted(cand_sc[...], n_top)
        z = jnp.ones_like(c[0])
        for kth in range(1, PEER_TOPK):
            z = z + jnp.exp(c[kth] - c[0])
        tau = 0.5 * (c[PEER_TOPK - 1] + c[PEER_TOPK])
        thr_ref[h] = tau - s[0]
        e1z_ref[h] = jnp.exp(s[0] - a[0]) / z
        e2 = jnp.exp(s[1] - b[0])
        for lg in range(s2_ref.shape[1]):
            s2_ref[h, lg] = s[1][:, lg * LANES:(lg + 1) * LANES]
            e2_ref[h, lg] = e2[:, lg * LANES:(lg + 1) * LANES]


def _peer_select(qp, keys, layer, tt):
    T = qp.shape[0]
    tt = _tile(T, tt)
    n_pairs = sum(1 for p in range(1, PEER_TOPK + 2) for q in range(1, PEER_TOPK + 2) if p * q <= PEER_TOPK + 1)
    out = jax.ShapeDtypeStruct((PEER_HEADS, N_KEYS, T), F32)
    spec = pl.BlockSpec((PEER_HEADS, N_KEYS, tt), lambda i: (0, 0, i))
    out_lg = jax.ShapeDtypeStruct((PEER_HEADS, T // LANES, N_KEYS, LANES), F32)
    spec_lg = pl.BlockSpec((PEER_HEADS, tt // LANES, N_KEYS, LANES), lambda i: (0, i, 0, 0))
    return pl.pallas_call(
        _peer_select_kernel,
        grid=(T // tt,),
        in_specs=[pl.BlockSpec((tt, PEER_HEADS * 2 * HEAD_DIM), lambda i: (i, 0)),
                  pl.BlockSpec((None, PEER_HEADS, 2, N_KEYS, HEAD_DIM), lambda i: (layer, 0, 0, 0, 0))],
        out_specs=[spec, spec, spec_lg, spec_lg],
        out_shape=[out, out, out_lg, out_lg],
        scratch_shapes=[pltpu.VMEM((8 * pl.next_power_of_2(-(-n_pairs // 8)), tt), F32)],
        compiler_params=_params("parallel"),
        name="peer_select",
    )(qp, keys)


def _peer_gate(thr_ref, e1z_ref, s2_ref, e2_ref, act_ref, hid_ref, rows, row0, slab, lg, sb):
    ls = slice(lg * LANES, (lg + 1) * LANES)
    js = slice(sb * slab, (sb + 1) * slab)
    gates = [None] * rows
    for h in range(PEER_HEADS):
        s2 = s2_ref[h, lg, js, :]
        e2 = e2_ref[h, lg, js, :]
        for ii in range(rows):
            thr = jnp.broadcast_to(thr_ref[h, row0 + ii:row0 + ii + 1, ls], (slab, LANES))
            e1z = jnp.broadcast_to(e1z_ref[h, row0 + ii:row0 + ii + 1, ls], (slab, LANES))
            g = jnp.where(s2 >= thr, e2 * e1z, 0.0)
            gates[ii] = g if gates[ii] is None else gates[ii] + g
    for ii in range(rows):
        es = slice(ii * N_KEYS + sb * slab, ii * N_KEYS + (sb + 1) * slab)
        a = act_ref[es, ls]
        hid_ref[es, ls] = (0.5 * a * (1.0 + lax.erf(a * (2.0 ** -0.5))) * gates[ii]).astype(BF16)


def _peer_mix_kernel(xt_ref, u_ref, vt_ref, thr_ref, e1z_ref, s2_ref, e2_ref, h_ref, g_ref, b_ref,
                     y_ref, acc_sc, act0, act1, hid0, hid1, *, rows, ne, alpha):
    f = pl.program_id(0)
    tt = xt_ref.shape[1]
    slab = GATE_ACC_VREGS * 8 // rows

    @pl.when(f == 0)
    def _():
        acc_sc[...] = jnp.zeros(acc_sc.shape, F32)
        for r in (act0, act1, hid0, hid1):
            r[...] = jnp.zeros(r.shape, r.dtype)

    def stages(act_w, act_r, hid_w, hid_r, row0):
        mt = MXU_TILE
        pieces = []
        m1 = MXU_TILE
        for n in range(tt // mt):
            for m in range(act_w.shape[0] // m1):
                def mm1(m=m, n=n):
                    act_w[m * m1:(m + 1) * m1, n * mt:(n + 1) * mt] = _dot(
                        u_ref[m * m1:(m + 1) * m1, :], xt_ref[:, n * mt:(n + 1) * mt])
                pieces.append((mm1, (xt_ref.shape[0] // mt) * (m1 // mt)))
        m2 = min(acc_sc.shape[0], PEER_MM_ROWS)
        for m in range(acc_sc.shape[0] // m2):
            for n in range(tt // mt):
                def mm2(m=m, n=n):
                    acc_sc[m * m2:(m + 1) * m2, n * mt:(n + 1) * mt] += _dot(
                        vt_ref[m * m2:(m + 1) * m2, :], hid_r[:, n * mt:(n + 1) * mt])
                pieces.append((mm2, (hid_r.shape[0] // mt) * (m2 // mt)))
        groups = [(lg, sb) for lg in range(tt // LANES) for sb in range(N_KEYS // slab)]
        total = sum(c for _, c in pieces)
        issued = 0
        for gi, (lg, sb) in enumerate(groups):
            while pieces and issued * len(groups) <= gi * total:
                fn, cost = pieces.pop(0)
                fn()
                issued += cost
            _peer_gate(thr_ref, e1z_ref, s2_ref, e2_ref, act_r, hid_w, rows, row0, slab, lg, sb)
        for fn, _ in pieces:
            fn()

    @pl.when(f % 2 == 0)
    def _():
        stages(act0, act1, hid1, hid0, rows % 8)

    @pl.when(f % 2 == 1)
    def _():
        stages(act1, act0, hid0, hid1, 0)

    @pl.when((f >= 2) & ((f - 2) % ne == ne - 1))
    def _():
        y_ref[...] = _layer_norm(alpha * h_ref[...] + acc_sc[...].T, g_ref[...], b_ref[...])
        acc_sc[...] = jnp.zeros(acc_sc.shape, F32)


def _peer_mix(xt, u, vt, sel, layer, h, ln_g, ln_b, alpha, tt, rows):
    D, T = xt.shape
    tt = _tile(T, tt)
    ne, eb = vt.shape[1], vt.shape[3]
    assert eb == rows * N_KEYS and ne * eb == u.shape[1] and rows in (4, 8) and ne % 2 == 0
    last = (T // tt) * ne - 1

    def tile_of(f):
        return jnp.clip(f, 0, last) // ne

    def block_of(f):
        return jnp.clip(f, 0, last) % ne

    row_spec = pl.BlockSpec((PEER_HEADS, None, 8, tt), lambda f: (0, block_of(f - 1) * rows // 8, 0, tile_of(f - 1)))
    full_spec = pl.BlockSpec((PEER_HEADS, tt // LANES, N_KEYS, LANES), lambda f: (0, tile_of(f - 1), 0, 0))
    thr, e1z, s2, e2 = sel
    thr = thr.reshape(PEER_HEADS, N_KEYS // 8, 8, T)
    e1z = e1z.reshape(PEER_HEADS, N_KEYS // 8, 8, T)
    return pl.pallas_call(
        functools.partial(_peer_mix_kernel, rows=rows, ne=ne, alpha=alpha),
        grid=(last + 3,),
        in_specs=[
            pl.BlockSpec((D, tt), lambda f: (0, tile_of(f))),
            pl.BlockSpec((None, eb, D), lambda f: (layer, block_of(f), 0)),
            pl.BlockSpec((None, None, D, eb), lambda f: (layer, block_of(f - 2), 0, 0)),
            row_spec, row_spec, full_spec, full_spec,
            pl.BlockSpec((tt, D), lambda f: (tile_of(f - 2), 0), pipeline_mode=pl.Buffered(1)),
            pl.BlockSpec((None, 1, D), lambda f: (layer, 0, 0)),
            pl.BlockSpec((None, 1, D), lambda f: (layer, 0, 0)),
        ],
        out_specs=pl.BlockSpec((tt, D), lambda f: (tile_of(f - 2), 0)),
        out_shape=jax.ShapeDtypeStruct((T, D), F32),
        scratch_shapes=[pltpu.VMEM((D, tt), F32), pltpu.VMEM((eb, tt), F32), pltpu.VMEM((eb, tt), F32),
                        pltpu.VMEM((eb, tt), BF16), pltpu.VMEM((eb, tt), BF16)],
        compiler_params=_params("arbitrary"),
        name="peer_mix",
    )(xt, u, vt, thr, e1z, s2, e2, h, ln_g, ln_b)


def kernel(x_prompt, x_sample, cache_self_k, cache_self_v, cache_mem_k, cache_mem_v, mem_prompt, w_in, w_o, w_mem_k, w_mem_v, rel_bias_table, diff_lambda, diff_subln_g, ln_g, ln_b, peer_w_q, peer_sub_keys, peer_u, peer_v):
    depth = w_in.shape[0]
    bp, n_prompt, d_model = x_prompt.shape
    bs, n_new, _ = x_sample.shape
    n_past = cache_self_k.shape[2]
    n_mem = mem_prompt.shape[1]
    alpha = (2 * depth) ** 0.25
    S = SELF_WIDTH

    w_in_b = w_in.astype(BF16)
    w_o_b = w_o.astype(BF16)
    w_mk_b = w_mem_k.astype(BF16)
    w_mv_b = w_mem_v.astype(BF16)
    w_pq_b = peer_w_q.astype(BF16)
    keys_b = peer_sub_keys.astype(BF16)
    u_b = peer_u.astype(BF16)
    eb = PEER_ROWS * N_KEYS
    vt_b = jnp.swapaxes(peer_v.reshape(depth, -1, eb, d_model), 2, 3).astype(BF16)
    ln_g4 = ln_g.reshape(depth, 2, 1, d_model)
    ln_b4 = ln_b.reshape(depth, 2, 1, d_model)
    cmk = cache_mem_k.reshape(depth, bs, n_mem, MEM_WIDTH)
    cmv = cache_mem_v.reshape(depth, bs, n_mem, MEM_WIDTH)
    mem2d = mem_prompt.reshape(bp * n_mem, d_model)

    tq_p = _tile(n_prompt, 512)
    tk_s = _tile(n_past, CACHE_TILE)
    nq_p = n_prompt // tq_p
    nk_s = n_past // tk_s
    i32 = jnp.int32
    bias_p = _bias_tiles(rel_bias_table, jnp.arange(nq_p, dtype=i32) * tq_p, jnp.zeros((nq_p,), i32), tq_p, tq_p)
    bias_sc = _bias_tiles(rel_bias_table, jnp.full((nk_s,), n_past, i32), jnp.arange(nk_s, dtype=i32) * tk_s, n_new, tk_s)
    bias_sn = _bias_tiles(rel_bias_table, jnp.full((1,), n_past, i32), jnp.full((1,), n_past, i32), n_new, n_new)

    hp = x_prompt.reshape(bp * n_prompt, d_model)
    hs = x_sample.reshape(bs * n_new, d_model)
    segs = ((0, S), (S, S), (2 * S, S), (3 * S, MEM_WIDTH))
    seg_dt = (BF16, F32, F32, BF16)
    new_k_p, new_v_p, new_mk_p, new_mv_p, new_k_s, new_v_s = [], [], [], [], [], []
    for i in range(depth):
        kind = i % 2
        j = i // 2
        qp, kp, vp, mqp = _linear(hp, w_in_b, i, segs, seg_dt, ROW_TILE)
        qs, ks, vs, mqs = _linear(hs, w_in_b, i, segs, seg_dt, ROW_TILE)
        (mkp,) = _linear(mem2d, w_mk_b, i, ((0, MEM_WIDTH),), (F32,), ROW_TILE)
        (mvp,) = _linear(mem2d, w_mv_b, i, ((0, MEM_WIDTH),), (F32,), ROW_TILE)
        qp3, kp3, vp3 = (a.reshape(bp, n_prompt, S) for a in (qp, kp, vp))
        qs3, ks3, vs3 = (a.reshape(bs, n_new, S) for a in (qs, ks, vs))

        if kind == 0:
            g = diff_subln_g[j].reshape(1, DIFF_VDIM)
            op = _diff_attention(qp3, kp3, vp3, bias_p, diff_lambda[j], g, i, tq_p, tq_p)
            os_ = _diff_attention(qs3, cache_self_k, cache_self_v, bias_sc, diff_lambda[j], g, i, n_new, tk_s,
                                  new=(ks3, vs3, bias_sn))
        else:
            op = _sb_attention(qp3, kp3, vp3, i, tq_p, tq_p)
            os_ = _sb_attention(qs3, cache_self_k, cache_self_v, i, n_new, tk_s, new=(ks3, vs3))

        mop = _mem_attention(mqp.reshape(bp, n_prompt, MEM_WIDTH), mkp.reshape(bp, n_mem, MEM_WIDTH),
                             mvp.reshape(bp, n_mem, MEM_WIDTH), tq_p)
        mos = _mem_attention(mqs.reshape(bs, n_new, MEM_WIDTH), cmk, cmv, n_new, layer=i)

        outs = []
        for h_res, o_self, o_mem in ((hp, op, mop), (hs, os_, mos)):
            T = h_res.shape[0]
            y, yt = _out_proj(o_self.reshape(T, S), o_mem.reshape(T, MEM_WIDTH), w_o_b, i, h_res,
                              ln_g4[:, 0], ln_b4[:, 0], alpha, ROW_TILE)
            (pq,) = _linear(y, w_pq_b, i, ((0, PEER_HEADS * 2 * HEAD_DIM),), (BF16,), ROW_TILE)
            sel = _peer_select(pq, keys_b, i, ROW_TILE)
            outs.append(_peer_mix(yt, u_b, vt_b, sel, i, y, ln_g4[:, 1], ln_b4[:, 1], alpha, ROW_TILE, PEER_ROWS))
        hp, hs = outs

        new_k_p.append(kp3)
        new_v_p.append(vp3)
        new_mk_p.append(mkp.reshape(bp, n_mem, MEM_HEADS, HEAD_DIM))
        new_mv_p.append(mvp.reshape(bp, n_mem, MEM_HEADS, HEAD_DIM))
        new_k_s.append(ks3)
        new_v_s.append(vs3)

    return (hp.reshape(bp, n_prompt, d_model), hs.reshape(bs, n_new, d_model),
            jnp.stack(new_k_p), jnp.stack(new_v_p), jnp.stack(new_mk_p), jnp.stack(new_mv_p),
            jnp.stack(new_k_s), jnp.stack(new_v_s))
```

```python
import functools
import math

import jax
import jax.numpy as jnp
from jax import lax
from jax.experimental import pallas as pl
from jax.experimental.pallas import tpu as pltpu

F32 = jnp.float32
BF16 = jnp.bfloat16

HEAD_DIM = 128
SELF_WIDTH = 1536
DIFF_HEADS = 6
DIFF_VDIM = 2 * HEAD_DIM
SB_HEADS = 12
MEM_HEADS = 4
MEM_WIDTH = MEM_HEADS * HEAD_DIM
CHUNK = 64
N_BUCKETS = 32
PEER_HEADS = 8
N_KEYS = 128
PEER_TOPK = 16
LN_EPS = 1e-5
SCALE = HEAD_DIM ** -0.5
NEG = -1e30

V7X_VMEM_LIMIT = 56 * 1024 * 1024
LANES = 128
MXU_TILE = 256
ROW_TILE = 512
CACHE_TILE = 2048
SB_PAIR = 2
PEER_MM_TILES = 4
PEER_ROWS = 4
GATE_ACC_VREGS = 16


def _params(*sem):
    return pltpu.CompilerParams(dimension_semantics=sem, vmem_limit_bytes=V7X_VMEM_LIMIT)


def _tile(n, t):
    t = min(n, t)
    assert n % t == 0, (n, t)
    return t


def _dot(a, b):
    return jnp.dot(a, b, preferred_element_type=F32)


def _dot_nt(a, b):
    return lax.dot_general(a, b, (((1,), (1,)), ((), ())), preferred_element_type=F32)


def _layer_norm(r, g, b):
    mu = jnp.mean(r, axis=-1, keepdims=True)
    d = r - mu
    var = jnp.mean(d * d, axis=-1, keepdims=True)
    return d * lax.rsqrt(var + LN_EPS) * g + b


def _linear_kernel(x_ref, w_ref, *o_refs, segs, nc):
    x = x_ref[...].astype(BF16)
    for o_ref, (start, width) in zip(o_refs, segs):
        for c in range(0, width, nc):
            o_ref[:, c:c + nc] = _dot(x, w_ref[:, start + c:start + c + nc]).astype(o_ref.dtype)


def _linear(x, w, layer, segs, dtypes, tm):
    M, K = x.shape
    N = w.shape[2]
    tm = _tile(M, tm)
    nc = 512
    assert all(wd % nc == 0 for _, wd in segs)
    return pl.pallas_call(
        functools.partial(_linear_kernel, segs=segs, nc=nc),
        grid=(M // tm,),
        in_specs=[
            pl.BlockSpec((tm, K), lambda i: (i, 0)),
            pl.BlockSpec((None, K, N), lambda i: (layer, 0, 0), pipeline_mode=pl.Buffered(1)),
        ],
        out_specs=[pl.BlockSpec((tm, wd), lambda i: (i, 0)) for _, wd in segs],
        out_shape=[jax.ShapeDtypeStruct((M, wd), dt) for (_, wd), dt in zip(segs, dtypes)],
        compiler_params=_params("parallel"),
        name="linear",
    )(x, w)


def _out_proj_kernel(o_ref, mo_ref, w_ref, h_ref, g_ref, b_ref, y_ref, yt_ref, *, alpha):
    acc = _dot(o_ref[...], w_ref[:SELF_WIDTH, :]) + _dot(mo_ref[...], w_ref[SELF_WIDTH:, :])
    y = _layer_norm(alpha * h_ref[...] + acc, g_ref[...], b_ref[...])
    y_ref[...] = y
    yt_ref[...] = y.T.astype(BF16)


def _out_proj(o, mo, w_o, layer, h, ln_g, ln_b, alpha, tm):
    T, D = h.shape
    tm = _tile(T, tm)
    return pl.pallas_call(
        functools.partial(_out_proj_kernel, alpha=alpha),
        grid=(T // tm,),
        in_specs=[
            pl.BlockSpec((tm, SELF_WIDTH), lambda i: (i, 0)),
            pl.BlockSpec((tm, MEM_WIDTH), lambda i: (i, 0)),
            pl.BlockSpec((None, SELF_WIDTH + MEM_WIDTH, D), lambda i: (layer, 0, 0), pipeline_mode=pl.Buffered(1)),
            pl.BlockSpec((tm, D), lambda i: (i, 0)),
            pl.BlockSpec((None, 1, D), lambda i: (layer, 0, 0)),
            pl.BlockSpec((None, 1, D), lambda i: (layer, 0, 0)),
        ],
        out_specs=[pl.BlockSpec((tm, D), lambda i: (i, 0)), pl.BlockSpec((D, tm), lambda i: (0, i))],
        out_shape=[jax.ShapeDtypeStruct((T, D), F32), jax.ShapeDtypeStruct((D, T), BF16)],
        compiler_params=_params("parallel"),
        name="out_proj_ln",
    )(o, mo, w_o, h, ln_g, ln_b)


def _bias_kernel(q0_ref, k0_ref, tab_ref, o_ref, *, tq, tk):
    h = pl.program_id(0)
    t = pl.program_id(1)
    q_pos = q0_ref[t] + lax.broadcasted_iota(jnp.int32, (tq, tk), 0)
    k_pos = k0_ref[t] + lax.broadcasted_iota(jnp.int32, (tq, tk), 1)
    rel = k_pos - q_pos
    n = jnp.abs(rel)
    large = jnp.full((tq, tk), 8, jnp.int32)
    for thr in (12, 16, 23, 32, 46, 64, 91):
        large = large + (n >= thr).astype(jnp.int32)
    bucket = jnp.where(rel > 0, N_BUCKETS // 2, 0) + jnp.where(n < 8, n, large)
    bias = jnp.zeros((tq, tk), F32)
    for b in range(N_BUCKETS):
        bias = jnp.where(bucket == b, tab_ref[b, h], bias)
    visible = (k_pos // CHUNK) <= (q_pos // CHUNK)
    o_ref[...] = jnp.where(visible, bias, NEG)


def _bias_tiles(rel_table, q0, k0, tq, tk):
    n = q0.shape[0]
    return pl.pallas_call(
        functools.partial(_bias_kernel, tq=tq, tk=tk),
        grid_spec=pltpu.PrefetchScalarGridSpec(
            num_scalar_prefetch=2,
            grid=(DIFF_HEADS, n),
            in_specs=[pl.BlockSpec(memory_space=pltpu.SMEM)],
            out_specs=pl.BlockSpec((None, None, tq, tk), lambda h, t, *_: (h, t, 0, 0)),
        ),
        out_shape=jax.ShapeDtypeStruct((DIFF_HEADS, n, tq, tk), F32),
        compiler_params=_params("parallel", "parallel"),
        name="bias_tiles",
    )(q0, k0, rel_table)


def _diff_step(q_ref, k_ref, v_ref, bias_ref, m_sc, l_sc, acc_sc):
    tq, tk = bias_ref.shape
    cw = min(tk, LANES)
    kb = k_ref[...].astype(BF16)
    v_ext = jnp.concatenate([v_ref[...].astype(BF16), jnp.ones((tk, LANES), BF16)], axis=1)
    live = [dict(), dict()]

    def scores(c):
        cs = slice(c * HEAD_DIM, (c + 1) * HEAD_DIM)
        live[c]["s"] = _dot_nt(q_ref[:, cs], kb[:, cs]) * SCALE + bias_ref[...]

    def probs(c):
        s = live[c].pop("s")
        m_prev = m_sc[c]
        m_new = jnp.maximum(m_prev, jnp.broadcast_to(jnp.max(s, axis=1, keepdims=True), (tq, LANES)))
        live[c]["alpha"] = jnp.exp(m_prev - m_new)
        m_sc[c] = m_new
        p = [jnp.exp(s[:, j:j + cw] - m_new[:, :cw]) for j in range(0, tk, cw)]
        live[c]["p"] = jnp.concatenate(p, axis=1).astype(BF16)

    def values(c):
        live[c]["pv"] = _dot(live[c].pop("p"), v_ext)

    def update(c):
        alpha = live[c].pop("alpha")
        pv = live[c].pop("pv")
        acc_sc[c] = jnp.concatenate([alpha, alpha], axis=1) * acc_sc[c] + pv[:, :DIFF_VDIM]
        l_sc[c] = alpha * l_sc[c] + pv[:, DIFF_VDIM:]

    stages = (scores, probs, values, update)
    for t in range(2 + len(stages) - 1):
        for s in reversed(range(len(stages))):
            if 0 <= t - s < 2:
                stages[s](t - s)


def _diff_kernel(qi_ref, kk_ref, fin_ref, *refs, has_new, lam_init):
    if has_new:
        lam_ref, g_ref, q_ref, k_ref, v_ref, bias_ref, kn_ref, vn_ref, biasn_ref, o_ref, m_sc, l_sc, acc_sc = refs
    else:
        lam_ref, g_ref, q_ref, k_ref, v_ref, bias_ref, o_ref, m_sc, l_sc, acc_sc = refs
    t = pl.program_id(2)
    kk = kk_ref[t]

    @pl.when(kk == 0)
    def _():
        m_sc[...] = jnp.full(m_sc.shape, NEG, F32)
        l_sc[...] = jnp.zeros(l_sc.shape, F32)
        acc_sc[...] = jnp.zeros(acc_sc.shape, F32)

    if has_new:
        @pl.when(kk == 0)
        def _():
            _diff_step(q_ref, kn_ref, vn_ref, biasn_ref, m_sc, l_sc, acc_sc)

        @pl.when(kk > 0)
        def _():
            _diff_step(q_ref, k_ref, v_ref, bias_ref, m_sc, l_sc, acc_sc)
    else:
        _diff_step(q_ref, k_ref, v_ref, bias_ref, m_sc, l_sc, acc_sc)

    @pl.when(fin_ref[t] == 1)
    def _():
        lp = lam_ref[...]
        lam = (jnp.exp(jnp.sum(lp[0:1] * lp[1:2], axis=1, keepdims=True))
               - jnp.exp(jnp.sum(lp[2:3] * lp[3:4], axis=1, keepdims=True)) + lam_init)
        l0 = jnp.concatenate([l_sc[0], l_sc[0]], axis=1)
        l1 = jnp.concatenate([l_sc[1], l_sc[1]], axis=1)
        o = acc_sc[0] / l0 - lam * (acc_sc[1] / l1)
        o = o * lax.rsqrt(jnp.mean(o * o, axis=1, keepdims=True) + LN_EPS) * g_ref[...]
        o_ref[...] = (o * (1.0 - lam_init)).astype(o_ref.dtype)


def _attention_steps(nq, nk, causal):
    steps = [(qi, kk) for qi in range(nq) for kk in range(qi + 1 if causal else nk)]
    qi_t = jnp.array([s[0] for s in steps], jnp.int32)
    kk_t = jnp.array([s[1] for s in steps], jnp.int32)
    fin_t = jnp.array([int(kk == (qi if causal else nk - 1)) for qi, kk in steps], jnp.int32)
    return qi_t, kk_t, fin_t


def _diff_attention(q, k, v, bias, lam_vec, subln_g, layer, tq, tk, new=None):
    B, Tq, _ = q.shape
    lam_init = 0.8 - 0.6 * math.exp(-0.3 * layer)
    has_new = new is not None
    if has_new:
        nkc = k.shape[2] // tk
        tables = _attention_steps(Tq // tq, nkc + 1, causal=False)
        kv_spec = pl.BlockSpec((None, None, tk, DIFF_VDIM),
                               lambda b, h, t, qi, kk, fin: (layer, b, jnp.clip(nkc - kk[t], 0, nkc - 1), h))
        bias_spec = pl.BlockSpec((None, None, tq, tk),
                                 lambda b, h, t, qi, kk, fin: (h, jnp.clip(nkc - kk[t], 0, nkc - 1), 0, 0))
        tn = new[0].shape[1]
        new_specs = [pl.BlockSpec((None, tn, DIFF_VDIM), lambda b, h, t, qi, kk, fin: (b, 0, h)),
                     pl.BlockSpec((None, tn, DIFF_VDIM), lambda b, h, t, qi, kk, fin: (b, 0, h)),
                     pl.BlockSpec((None, None, tq, tn), lambda b, h, t, qi, kk, fin: (h, 0, 0, 0))]
        new_args = list(new)
    else:
        tables = _attention_steps(Tq // tq, k.shape[1] // tk, causal=True)
        kv_spec = pl.BlockSpec((None, tk, DIFF_VDIM), lambda b, h, t, qi, kk, fin: (b, qi[t] - kk[t], h))
        bias_spec = pl.BlockSpec((None, None, tq, tk), lambda b, h, t, qi, kk, fin: (h, kk[t], 0, 0))
        new_specs, new_args = [], []
    return pl.pallas_call(
        functools.partial(_diff_kernel, has_new=has_new, lam_init=lam_init),
        grid_spec=pltpu.PrefetchScalarGridSpec(
            num_scalar_prefetch=3,
            grid=(B, DIFF_HEADS, tables[0].shape[0]),
            in_specs=[
                pl.BlockSpec((4, HEAD_DIM), lambda b, h, t, qi, kk, fin: (0, 0)),
                pl.BlockSpec((1, DIFF_VDIM), lambda b, h, t, qi, kk, fin: (0, 0)),
                pl.BlockSpec((None, tq, DIFF_VDIM), lambda b, h, t, qi, kk, fin: (b, qi[t], h)),
                kv_spec, kv_spec, bias_spec, *new_specs,
            ],
            out_specs=pl.BlockSpec((None, tq, DIFF_VDIM), lambda b, h, t, qi, kk, fin: (b, qi[t], h)),
            scratch_shapes=[pltpu.VMEM((2, tq, LANES), F32), pltpu.VMEM((2, tq, LANES), F32),
                            pltpu.VMEM((2, tq, DIFF_VDIM), F32)],
        ),
        out_shape=jax.ShapeDtypeStruct((B, Tq, SELF_WIDTH), BF16),
        compiler_params=_params("parallel", "parallel", "arbitrary"),
        name="diff_attention",
    )(*tables, lam_vec, subln_g, q, k, v, bias, *new_args)


def _sb_step(q_ref, k_ref, v_ref, q_pos0, k_pos0, acc_sc, r_sc, cw, masked):
    tq = q_ref.shape[0]
    tk = k_ref.shape[0]
    cw = min(cw, tk)
    later = (lax.broadcasted_iota(jnp.int32, (cw, cw), 0) > lax.broadcasted_iota(jnp.int32, (cw, cw), 1))
    later = jnp.where(later, 1.0, 0.0).astype(BF16)
    units = [(c0, hh) for c0 in range(tk - cw, -1, -cw) for hh in range(SB_PAIR)]
    live = [dict() for _ in units]

    def scores(u):
        c0, hh = units[u]
        hs = slice(hh * HEAD_DIM, (hh + 1) * HEAD_DIM)
        live[u]["z"] = _dot_nt(q_ref[:, hs], k_ref[c0:c0 + cw, hs].astype(BF16)) * SCALE

    def logs(u):
        c0, _ = units[u]
        z = live[u].pop("z")
        log_keep = -(jnp.maximum(z, 0.0) + jnp.log(1.0 + jnp.exp(-jnp.abs(z))))
        live[u]["log_beta"] = log_keep + z
        if masked:
            q_pos = q_pos0 + lax.broadcasted_iota(jnp.int32, (tq, cw), 0)
            before = (k_pos0 + c0 + lax.broadcasted_iota(jnp.int32, (tq, cw), 1)) < q_pos
            log_keep = jnp.where(before, log_keep, 0.0)
            live[u]["before"] = before
        live[u]["log_keep"] = log_keep

    def cumsum(u):
        live[u]["cum"] = _dot(live[u]["log_keep"].astype(BF16), later)

    def weights(u):
        _, hh = units[u]
        a = jnp.exp(live[u].pop("log_beta") + (live[u].pop("cum") + r_sc[hh]))
        if masked:
            a = jnp.where(live[u].pop("before"), a, 0.0)
        live[u]["a"] = a.astype(BF16)
        r_sc[hh] += jnp.sum(live[u].pop("log_keep"), axis=1, keepdims=True)

    def values(u):
        c0, hh = units[u]
        hs = slice(hh * HEAD_DIM, (hh + 1) * HEAD_DIM)
        acc_sc[hh] += _dot(live[u].pop("a"), v_ref[c0:c0 + cw, hs].astype(BF16))

    stages = (scores, logs, cumsum, weights, values)
    for t in range(len(units) + len(stages) - 1):
        for s in reversed(range(len(stages))):
            if 0 <= t - s < len(units):
                stages[s](t - s)


def _sb_kernel(qi_ref, kk_ref, fin_ref, *refs, has_new, tq, tk, q_off, nkc, cw):
    if has_new:
        q_ref, k_ref, v_ref, kn_ref, vn_ref, o_ref, acc_sc, r_sc = refs
    else:
        q_ref, k_ref, v_ref, o_ref, acc_sc, r_sc = refs
    t = pl.program_id(2)
    qi = qi_ref[t]
    kk = kk_ref[t]

    @pl.when(kk == 0)
    def _():
        acc_sc[...] = jnp.zeros(acc_sc.shape, F32)
        r_sc[...] = jnp.zeros(r_sc.shape, F32)

    if has_new:
        @pl.when(kk == 0)
        def _():
            _sb_step(q_ref, kn_ref, vn_ref, q_off, q_off, acc_sc, r_sc, cw, True)

        @pl.when(kk > 0)
        def _():
            _sb_step(q_ref, k_ref, v_ref, q_off, (nkc - kk) * tk, acc_sc, r_sc, cw, False)
    else:
        @pl.when(kk == 0)
        def _():
            _sb_step(q_ref, k_ref, v_ref, qi * tq, qi * tk, acc_sc, r_sc, cw, True)

        @pl.when(kk > 0)
        def _():
            _sb_step(q_ref, k_ref, v_ref, qi * tq, (qi - kk) * tk, acc_sc, r_sc, cw, False)

    @pl.when(fin_ref[t] == 1)
    def _():
        for hh in range(SB_PAIR):
            o_ref[:, hh * HEAD_DIM:(hh + 1) * HEAD_DIM] = acc_sc[hh].astype(o_ref.dtype)


def _sb_attention(q, k, v, layer, tq, tk, new=None):
    B, Tq, _ = q.shape
    has_new = new is not None
    hw = SB_PAIR * HEAD_DIM
    if has_new:
        nkc = k.shape[2] // tk
        q_off = k.shape[2]
        tables = _attention_steps(Tq // tq, nkc + 1, causal=False)
        kv_spec = pl.BlockSpec((None, None, tk, hw),
                               lambda b, h, t, qi, kk, fin: (layer, b, jnp.clip(nkc - kk[t], 0, nkc - 1), h))
        tn = new[0].shape[1]
        new_specs = [pl.BlockSpec((None, tn, hw), lambda b, h, t, qi, kk, fin: (b, 0, h))] * 2
        new_args = list(new)
    else:
        assert tq == tk
        nkc = k.shape[1] // tk
        q_off = 0
        tables = _attention_steps(Tq // tq, nkc, causal=True)
        kv_spec = pl.BlockSpec((None, tk, hw), lambda b, h, t, qi, kk, fin: (b, qi[t] - kk[t], h))
        new_specs, new_args = [], []
    return pl.pallas_call(
        functools.partial(_sb_kernel, has_new=has_new, tq=tq, tk=tk, q_off=q_off, nkc=nkc, cw=MXU_TILE),
        grid_spec=pltpu.PrefetchScalarGridSpec(
            num_scalar_prefetch=3,
            grid=(B, SB_HEADS // SB_PAIR, tables[0].shape[0]),
            in_specs=[pl.BlockSpec((None, tq, hw), lambda b, h, t, qi, kk, fin: (b, qi[t], h)),
                      kv_spec, kv_spec, *new_specs],
            out_specs=pl.BlockSpec((None, tq, hw), lambda b, h, t, qi, kk, fin: (b, qi[t], h)),
            scratch_shapes=[pltpu.VMEM((SB_PAIR, tq, HEAD_DIM), F32), pltpu.VMEM((SB_PAIR, tq, 1), F32)],
        ),
        out_shape=jax.ShapeDtypeStruct((B, Tq, SELF_WIDTH), BF16),
        compiler_params=_params("parallel", "parallel", "arbitrary"),
        name="sb_attention",
    )(*tables, q, k, v, *new_args)


def _mem_kernel(q_ref, k_ref, v_ref, o_ref):
    q = q_ref[...]
    kb = k_ref[...].astype(BF16)
    vb = v_ref[...].astype(BF16)
    for h in range(MEM_HEADS):
        sl = slice(h * HEAD_DIM, (h + 1) * HEAD_DIM)
        s = _dot_nt(q[:, sl], kb[:, sl]) * SCALE
        p = jnp.exp(s - jnp.max(s, axis=1, keepdims=True))
        o = _dot(p.astype(BF16), vb[:, sl]) / jnp.sum(p, axis=1, keepdims=True)
        o_ref[:, sl] = o.astype(o_ref.dtype)


def _mem_attention(mq, mk, mv, tq, layer=None):
    B, Tq, _ = mq.shape
    n_mem = mk.shape[-2]
    if layer is None:
        kv_spec = pl.BlockSpec((None, n_mem, MEM_WIDTH), lambda b, qi: (b, 0, 0))
    else:
        kv_spec = pl.BlockSpec((None, None, n_mem, MEM_WIDTH), lambda b, qi: (layer, b, 0, 0))
    return pl.pallas_call(
        _mem_kernel,
        grid=(B, Tq // tq),
        in_specs=[pl.BlockSpec((None, tq, MEM_WIDTH), lambda b, qi: (b, qi, 0)), kv_spec, kv_spec],
        out_specs=pl.BlockSpec((None, tq, MEM_WIDTH), lambda b, qi: (b, qi, 0)),
        out_shape=jax.ShapeDtypeStruct((B, Tq, MEM_WIDTH), BF16),
        compiler_params=_params("parallel", "parallel"),
        name="mem_attention",
    )(mq, mk, mv)


def _top_values(x, n):
    out = []
    for _ in range(n):
        m = jnp.max(x, axis=0, keepdims=True)
        out.append(m)
        x = jnp.where(x == m, -jnp.inf, x)
    return out


def _batcher_pairs(n):
    pairs = []
    p = 1
    while p < n:
        k = p
        while k >= 1:
            for j in range(k % p, n - k, 2 * k):
                for i in range(min(k, n - j - k)):
                    if (i + j) // (2 * p) == (i + j + k) // (2 * p):
                        pairs.append((i + j, i + j + k))
            k //= 2
        p *= 2
    return pairs


def _top_values_sorted(x, n):
    g = x.shape[0] // 8
    cols = []
    for lg in range(x.shape[1] // LANES):
        v = [x[r * 8:(r + 1) * 8, lg * LANES:(lg + 1) * LANES] for r in range(g)]
        for i, j in _batcher_pairs(g):
            v[i], v[j] = jnp.maximum(v[i], v[j]), jnp.minimum(v[i], v[j])
        v.append(jnp.full_like(v[0], -jnp.inf))
        out = []
        for k in range(n):
            m = jnp.max(v[0], axis=0, keepdims=True)
            out.append(m)
            popped = v[0] == m
            for d in range(min(n - 1 - k, g)):
                v[d] = jnp.where(popped, v[d + 1], v[d])
        cols.append(out)
    return [jnp.concatenate([c[k] for c in cols], axis=1) for k in range(n)]


def _peer_select_kernel(q_ref, keys_ref, thr_ref, e1z_ref, s2_ref, e2_ref, cand_sc):
    n_top = PEER_TOPK + 1
    pairs = [(p, q) for p in range(n_top) for q in range(n_top) if (p + 1) * (q + 1) <= n_top]
    for h in range(PEER_HEADS):
        s = []
        for c in range(2):
            col = (2 * h + c) * HEAD_DIM
            s.append(_dot_nt(keys_ref[h, c], q_ref[:, col:col + HEAD_DIM]))
        a = _top_values_sorted(s[0], n_top)
        b = _top_values_sorted(s[1], n_top)
        cand_sc[...] = jnp.full(cand_sc.shape, -jnp.inf, F32)
        for r, (p, q) in enumerate(pairs):
            cand_sc[r:r + 1, :] = a[p] + b[q]
        c = _top_values_sorted(cand_sc[...], n_top)
        z = jnp.ones_like(c[0])
        for kth in range(1, PEER_TOPK):
            z = z + jnp.exp(c[kth] - c[0])
        tau = 0.5 * (c[PEER_TOPK - 1] + c[PEER_TOPK])
        thr_ref[h] = tau - s[0]
        e1z_ref[h] = jnp.exp(s[0] - a[0]) / z
        e2 = jnp.exp(s[1] - b[0])
        for lg in range(s2_ref.shape[1]):
            s2_ref[h, lg] = s[1][:, lg * LANES:(lg + 1) * LANES]
            e2_ref[h, lg] = e2[:, lg * LANES:(lg + 1) * LANES]


def _peer_select(qp, keys, layer, tt):
    T = qp.shape[0]
    tt = _tile(T, tt)
    n_pairs = sum(1 for p in range(1, PEER_TOPK + 2) for q in range(1, PEER_TOPK + 2) if p * q <= PEER_TOPK + 1)
    out = jax.ShapeDtypeStruct((PEER_HEADS, N_KEYS, T), F32)
    spec = pl.BlockSpec((PEER_HEADS, N_KEYS, tt), lambda i: (0, 0, i))
    out_lg = jax.ShapeDtypeStruct((PEER_HEADS, T // LANES, N_KEYS, LANES), F32)
    spec_lg = pl.BlockSpec((PEER_HEADS, tt // LANES, N_KEYS, LANES), lambda i: (0, i, 0, 0))
    return pl.pallas_call(
        _peer_select_kernel,
        grid=(T // tt,),
        in_specs=[pl.BlockSpec((tt, PEER_HEADS * 2 * HEAD_DIM), lambda i: (i, 0)),
                  pl.BlockSpec((None, PEER_HEADS, 2, N_KEYS, HEAD_DIM), lambda i: (layer, 0, 0, 0, 0))],
        out_specs=[spec, spec, spec_lg, spec_lg],
        out_shape=[out, out, out_lg, out_lg],
        scratch_shapes=[pltpu.VMEM((8 * pl.next_power_of_2(-(-n_pairs // 8)), tt), F32)],
        compiler_params=_params("parallel"),
        name="peer_select",
    )(qp, keys)


def _peer_gate(thr_ref, e1z_ref, s2_ref, e2_ref, act_ref, hid_ref, rows, row0, slab, lg, sb):
    ls = slice(lg * LANES, (lg + 1) * LANES)
    js = slice(sb * slab, (sb + 1) * slab)
    gates = [None] * rows
    for h in range(PEER_HEADS):
        s2 = s2_ref[h, lg, js, :]
        e2 = e2_ref[h, lg, js, :]
        for ii in range(rows):
            thr = jnp.broadcast_to(thr_ref[h, row0 + ii:row0 + ii + 1, ls], (slab, LANES))
            e1z = jnp.broadcast_to(e1z_ref[h, row0 + ii:row0 + ii + 1, ls], (slab, LANES))
            g = jnp.where(s2 >= thr, e2 * e1z, 0.0)
            gates[ii] = g if gates[ii] is None else gates[ii] + g
    for ii in range(rows):
        es = slice(ii * N_KEYS + sb * slab, ii * N_KEYS + (sb + 1) * slab)
        a = act_ref[es, ls]
        hid_ref[es, ls] = (0.5 * a * (1.0 + lax.erf(a * (2.0 ** -0.5))) * gates[ii]).astype(BF16)


def _peer_mix_kernel(xt_ref, u_ref, vt_ref, thr_ref, e1z_ref, s2_ref, e2_ref, h_ref, g_ref, b_ref,
                     y_ref, acc_sc, act0, act1, hid0, hid1, *, rows, ne, alpha):
    f = pl.program_id(0)
    tt = xt_ref.shape[1]
    slab = GATE_ACC_VREGS * 8 // rows

    @pl.when(f == 0)
    def _():
        acc_sc[...] = jnp.zeros(acc_sc.shape, F32)
        for r in (act0, act1, hid0, hid1):
            r[...] = jnp.zeros(r.shape, r.dtype)

    def stages(act_w, act_r, hid_w, hid_r, row0):
        mt = MXU_TILE
        pieces = []
        m1 = mt
        kp = PEER_MM_TILES * mt
        for n in range(tt // mt):
            for m in range(act_w.shape[0] // m1):
                for k in range(xt_ref.shape[0] // kp):
                    def mm1(m=m, n=n, k=k):
                        d = _dot(u_ref[m * m1:(m + 1) * m1, k * kp:(k + 1) * kp],
                                 xt_ref[k * kp:(k + 1) * kp, n * mt:(n + 1) * mt])
                        if k == 0:
                            act_w[m * m1:(m + 1) * m1, n * mt:(n + 1) * mt] = d
                        else:
                            act_w[m * m1:(m + 1) * m1, n * mt:(n + 1) * mt] += d
                    pieces.append((mm1, PEER_MM_TILES))
        m2 = mt
        mn = min(tt, (PEER_MM_TILES * mt * mt) // hid_r.shape[0])
        for m in range(acc_sc.shape[0] // m2):
            for n in range(tt // mn):
                def mm2(m=m, n=n):
                    acc_sc[m * m2:(m + 1) * m2, n * mn:(n + 1) * mn] += _dot(
                        vt_ref[m * m2:(m + 1) * m2, :], hid_r[:, n * mn:(n + 1) * mn])
                pieces.append((mm2, (hid_r.shape[0] // mt) * (mn // mt)))
        groups = [(lg, sb) for lg in range(tt // LANES) for sb in range(N_KEYS // slab)]
        total = sum(c for _, c in pieces)
        issued = 0
        for gi, (lg, sb) in enumerate(groups):
            while pieces and issued * len(groups) <= gi * total:
                fn, cost = pieces.pop(0)
                fn()
                issued += cost
            _peer_gate(thr_ref, e1z_ref, s2_ref, e2_ref, act_r, hid_w, rows, row0, slab, lg, sb)
        for fn, _ in pieces:
            fn()

    @pl.when(f % 2 == 0)
    def _():
        stages(act0, act1, hid1, hid0, rows % 8)

    @pl.when(f % 2 == 1)
    def _():
        stages(act1, act0, hid0, hid1, 0)

    @pl.when((f >= 2) & ((f - 2) % ne == ne - 1))
    def _():
        y_ref[...] = _layer_norm(alpha * h_ref[...] + acc_sc[...].T, g_ref[...], b_ref[...])
        acc_sc[...] = jnp.zeros(acc_sc.shape, F32)


def _peer_mix(xt, u, vt, sel, layer, h, ln_g, ln_b, alpha, tt, rows):
    D, T = xt.shape
    tt = _tile(T, tt)
    ne, eb = vt.shape[1], vt.shape[3]
    assert eb == rows * N_KEYS and ne * eb == u.shape[1] and rows in (4, 8) and ne % 2 == 0
    last = (T // tt) * ne - 1

    def tile_of(f):
        return jnp.clip(f, 0, last) // ne

    def block_of(f):
        return jnp.clip(f, 0, last) % ne

    row_spec = pl.BlockSpec((PEER_HEADS, None, 8, tt), lambda f: (0, block_of(f - 1) * rows // 8, 0, tile_of(f - 1)))
    full_spec = pl.BlockSpec((PEER_HEADS, tt // LANES, N_KEYS, LANES), lambda f: (0, tile_of(f - 1), 0, 0))
    thr, e1z, s2, e2 = sel
    thr = thr.reshape(PEER_HEADS, N_KEYS // 8, 8, T)
    e1z = e1z.reshape(PEER_HEADS, N_KEYS // 8, 8, T)
    return pl.pallas_call(
        functools.partial(_peer_mix_kernel, rows=rows, ne=ne, alpha=alpha),
        grid=(last + 3,),
        in_specs=[
            pl.BlockSpec((D, tt), lambda f: (0, tile_of(f))),
            pl.BlockSpec((None, eb, D), lambda f: (layer, block_of(f), 0)),
            pl.BlockSpec((None, None, D, eb), lambda f: (layer, block_of(f - 2), 0, 0)),
            row_spec, row_spec, full_spec, full_spec,
            pl.BlockSpec((tt, D), lambda f: (tile_of(f - 2), 0), pipeline_mode=pl.Buffered(1)),
            pl.BlockSpec((None, 1, D), lambda f: (layer, 0, 0)),
            pl.BlockSpec((None, 1, D), lambda f: (layer, 0, 0)),
        ],
        out_specs=pl.BlockSpec((tt, D), lambda f: (tile_of(f - 2), 0)),
        out_shape=jax.ShapeDtypeStruct((T, D), F32),
        scratch_shapes=[pltpu.VMEM((D, tt), F32), pltpu.VMEM((eb, tt), F32), pltpu.VMEM((eb, tt), F32),
                        pltpu.VMEM((eb, tt), BF16), pltpu.VMEM((eb, tt), BF16)],
        compiler_params=_params("arbitrary"),
        name="peer_mix",
    )(xt, u, vt, thr, e1z, s2, e2, h, ln_g, ln_b)


def kernel(x_prompt, x_sample, cache_self_k, cache_self_v, cache_mem_k, cache_mem_v, mem_prompt, w_in, w_o, w_mem_k, w_mem_v, rel_bias_table, diff_lambda, diff_subln_g, ln_g, ln_b, peer_w_q, peer_sub_keys, peer_u, peer_v):
    depth = w_in.shape[0]
    bp, n_prompt, d_model = x_prompt.shape
    bs, n_new, _ = x_sample.shape
    n_past = cache_self_k.shape[2]
    n_mem = mem_prompt.shape[1]
    alpha = (2 * depth) ** 0.25
    S = SELF_WIDTH

    w_in_b = w_in.astype(BF16)
    w_o_b = w_o.astype(BF16)
    w_mk_b = w_mem_k.astype(BF16)
    w_mv_b = w_mem_v.astype(BF16)
    w_pq_b = peer_w_q.astype(BF16)
    keys_b = peer_sub_keys.astype(BF16)
    u_b = peer_u.astype(BF16)
    eb = PEER_ROWS * N_KEYS
    vt_b = jnp.swapaxes(peer_v.reshape(depth, -1, eb, d_model), 2, 3).astype(BF16)
    ln_g4 = ln_g.reshape(depth, 2, 1, d_model)
    ln_b4 = ln_b.reshape(depth, 2, 1, d_model)
    cmk = cache_mem_k.reshape(depth, bs, n_mem, MEM_WIDTH)
    cmv = cache_mem_v.reshape(depth, bs, n_mem, MEM_WIDTH)
    mem2d = mem_prompt.reshape(bp * n_mem, d_model)

    tq_p = _tile(n_prompt, 512)
    tk_s = _tile(n_past, CACHE_TILE)
    nq_p = n_prompt // tq_p
    nk_s = n_past // tk_s
    i32 = jnp.int32
    bias_p = _bias_tiles(rel_bias_table, jnp.arange(nq_p, dtype=i32) * tq_p, jnp.zeros((nq_p,), i32), tq_p, tq_p)
    bias_sc = _bias_tiles(rel_bias_table, jnp.full((nk_s,), n_past, i32), jnp.arange(nk_s, dtype=i32) * tk_s, n_new, tk_s)
    bias_sn = _bias_tiles(rel_bias_table, jnp.full((1,), n_past, i32), jnp.full((1,), n_past, i32), n_new, n_new)

    hp = x_prompt.reshape(bp * n_prompt, d_model)
    hs = x_sample.reshape(bs * n_new, d_model)
    segs = ((0, S), (S, S), (2 * S, S), (3 * S, MEM_WIDTH))
    seg_dt = (BF16, F32, F32, BF16)
    new_k_p, new_v_p, new_mk_p, new_mv_p, new_k_s, new_v_s = [], [], [], [], [], []
    for i in range(depth):
        kind = i % 2
        j = i // 2
        qp, kp, vp, mqp = _linear(hp, w_in_b, i, segs, seg_dt, ROW_TILE)
        qs, ks, vs, mqs = _linear(hs, w_in_b, i, segs, seg_dt, ROW_TILE)
        (mkp,) = _linear(mem2d, w_mk_b, i, ((0, MEM_WIDTH),), (F32,), ROW_TILE)
        (mvp,) = _linear(mem2d, w_mv_b, i, ((0, MEM_WIDTH),), (F32,), ROW_TILE)
        qp3, kp3, vp3 = (a.reshape(bp, n_prompt, S) for a in (qp, kp, vp))
        qs3, ks3, vs3 = (a.reshape(bs, n_new, S) for a in (qs, ks, vs))

        if kind == 0:
            g = diff_subln_g[j].reshape(1, DIFF_VDIM)
            op = _diff_attention(qp3, kp3, vp3, bias_p, diff_lambda[j], g, i, tq_p, tq_p)
            os_ = _diff_attention(qs3, cache_self_k, cache_self_v, bias_sc, diff_lambda[j], g, i, n_new, tk_s,
                                  new=(ks3, vs3, bias_sn))
        else:
            op = _sb_attention(qp3, kp3, vp3, i, tq_p, tq_p)
            os_ = _sb_attention(qs3, cache_self_k, cache_self_v, i, n_new, tk_s, new=(ks3, vs3))

        mop = _mem_attention(mqp.reshape(bp, n_prompt, MEM_WIDTH), mkp.reshape(bp, n_mem, MEM_WIDTH),
                             mvp.reshape(bp, n_mem, MEM_WIDTH), tq_p)
        mos = _mem_attention(mqs.reshape(bs, n_new, MEM_WIDTH), cmk, cmv, n_new, layer=i)

        outs = []
        for h_res, o_self, o_mem in ((hp, op, mop), (hs, os_, mos)):
            T = h_res.shape[0]
            y, yt = _out_proj(o_self.reshape(T, S), o_mem.reshape(T, MEM_WIDTH), w_o_b, i, h_res,
                              ln_g4[:, 0], ln_b4[:, 0], alpha, ROW_TILE)
            (pq,) = _linear(y, w_pq_b, i, ((0, PEER_HEADS * 2 * HEAD_DIM),), (BF16,), ROW_TILE)
            sel = _peer_select(pq, keys_b, i, ROW_TILE)
            outs.append(_peer_mix(yt, u_b, vt_b, sel, i, y, ln_g4[:, 1], ln_b4[:, 1], alpha, ROW_TILE, PEER_ROWS))
        hp, hs = outs

        new_k_p.append(kp3)
        new_v_p.append(vp3)
        new_mk_p.append(mkp.reshape(bp, n_mem, MEM_HEADS, HEAD_DIM))
        new_mv_p.append(mvp.reshape(bp, n_mem, MEM_HEADS, HEAD_DIM))
        new_k_s.append(ks3)
        new_v_s.append(vs3)

    return (hp.reshape(bp, n_prompt, d_model), hs.reshape(bs, n_new, d_model),
            jnp.stack(new_k_p), jnp.stack(new_v_p), jnp.stack(new_mk_p), jnp.stack(new_mv_p),
            jnp.stack(new_k_s), jnp.stack(new_v_s))
```

```python
import functools
import math

import jax
import jax.numpy as jnp
from jax import lax
from jax.experimental import pallas as pl
from jax.experimental.pallas import tpu as pltpu

F32 = jnp.float32
BF16 = jnp.bfloat16

HEAD_DIM = 128
SELF_WIDTH = 1536
DIFF_HEADS = 6
DIFF_VDIM = 2 * HEAD_DIM
SB_HEADS = 12
MEM_HEADS = 4
MEM_WIDTH = MEM_HEADS * HEAD_DIM
CHUNK = 64
N_BUCKETS = 32
PEER_HEADS = 8
N_KEYS = 128
PEER_TOPK = 16
LN_EPS = 1e-5
SCALE = HEAD_DIM ** -0.5
NEG = -1e30

V7X_VMEM_LIMIT = 56 * 1024 * 1024
LANES = 128
MXU_TILE = 256
ROW_TILE = 512
CACHE_TILE = 2048
SB_PAIR = 2
PEER_MM_TILES = 4
PEER_ROWS = 4
GATE_ACC_VREGS = 16


def _params(*sem):
    return pltpu.CompilerParams(dimension_semantics=sem, vmem_limit_bytes=V7X_VMEM_LIMIT)


def _tile(n, t):
    t = min(n, t)
    assert n % t == 0, (n, t)
    return t


def _dot(a, b):
    return jnp.dot(a, b, preferred_element_type=F32)


def _dot_nt(a, b):
    return lax.dot_general(a, b, (((1,), (1,)), ((), ())), preferred_element_type=F32)


def _layer_norm(r, g, b):
    mu = jnp.mean(r, axis=-1, keepdims=True)
    d = r - mu
    var = jnp.mean(d * d, axis=-1, keepdims=True)
    return d * lax.rsqrt(var + LN_EPS) * g + b


def _linear_kernel(x_ref, w_ref, *o_refs, segs, nc):
    x = x_ref[...].astype(BF16)
    for o_ref, (start, width) in zip(o_refs, segs):
        for c in range(0, width, nc):
            o_ref[:, c:c + nc] = _dot(x, w_ref[:, start + c:start + c + nc]).astype(o_ref.dtype)


def _linear(x, w, layer, segs, dtypes, tm):
    M, K = x.shape
    N = w.shape[2]
    tm = _tile(M, tm)
    nc = 512
    assert all(wd % nc == 0 for _, wd in segs)
    return pl.pallas_call(
        functools.partial(_linear_kernel, segs=segs, nc=nc),
        grid=(M // tm,),
        in_specs=[
            pl.BlockSpec((tm, K), lambda i: (i, 0)),
            pl.BlockSpec((None, K, N), lambda i: (layer, 0, 0), pipeline_mode=pl.Buffered(1)),
        ],
        out_specs=[pl.BlockSpec((tm, wd), lambda i: (i, 0)) for _, wd in segs],
        out_shape=[jax.ShapeDtypeStruct((M, wd), dt) for (_, wd), dt in zip(segs, dtypes)],
        compiler_params=_params("parallel"),
        name="linear",
    )(x, w)


def _out_proj_kernel(o_ref, mo_ref, w_ref, h_ref, g_ref, b_ref, y_ref, yt_ref, *, alpha):
    acc = _dot(o_ref[...], w_ref[:SELF_WIDTH, :]) + _dot(mo_ref[...], w_ref[SELF_WIDTH:, :])
    y = _layer_norm(alpha * h_ref[...] + acc, g_ref[...], b_ref[...])
    y_ref[...] = y
    yt_ref[...] = y.T.astype(BF16)


def _out_proj(o, mo, w_o, layer, h, ln_g, ln_b, alpha, tm):
    T, D = h.shape
    tm = _tile(T, tm)
    return pl.pallas_call(
        functools.partial(_out_proj_kernel, alpha=alpha),
        grid=(T // tm,),
        in_specs=[
            pl.BlockSpec((tm, SELF_WIDTH), lambda i: (i, 0)),
            pl.BlockSpec((tm, MEM_WIDTH), lambda i: (i, 0)),
            pl.BlockSpec((None, SELF_WIDTH + MEM_WIDTH, D), lambda i: (layer, 0, 0), pipeline_mode=pl.Buffered(1)),
            pl.BlockSpec((tm, D), lambda i: (i, 0)),
            pl.BlockSpec((None, 1, D), lambda i: (layer, 0, 0)),
            pl.BlockSpec((None, 1, D), lambda i: (layer, 0, 0)),
        ],
        out_specs=[pl.BlockSpec((tm, D), lambda i: (i, 0)), pl.BlockSpec((D, tm), lambda i: (0, i))],
        out_shape=[jax.ShapeDtypeStruct((T, D), F32), jax.ShapeDtypeStruct((D, T), BF16)],
        compiler_params=_params("parallel"),
        name="out_proj_ln",
    )(o, mo, w_o, h, ln_g, ln_b)


def _bias_kernel(q0_ref, k0_ref, tab_ref, o_ref, *, tq, tk):
    h = pl.program_id(0)
    t = pl.program_id(1)
    q_pos = q0_ref[t] + lax.broadcasted_iota(jnp.int32, (tq, tk), 0)
    k_pos = k0_ref[t] + lax.broadcasted_iota(jnp.int32, (tq, tk), 1)
    rel = k_pos - q_pos
    n = jnp.abs(rel)
    large = jnp.full((tq, tk), 8, jnp.int32)
    for thr in (12, 16, 23, 32, 46, 64, 91):
        large = large + (n >= thr).astype(jnp.int32)
    bucket = jnp.where(rel > 0, N_BUCKETS // 2, 0) + jnp.where(n < 8, n, large)
    bias = jnp.zeros((tq, tk), F32)
    for b in range(N_BUCKETS):
        bias = jnp.where(bucket == b, tab_ref[b, h], bias)
    visible = (k_pos // CHUNK) <= (q_pos // CHUNK)
    o_ref[...] = jnp.where(visible, bias, NEG)


def _bias_tiles(rel_table, q0, k0, tq, tk):
    n = q0.shape[0]
    return pl.pallas_call(
        functools.partial(_bias_kernel, tq=tq, tk=tk),
        grid_spec=pltpu.PrefetchScalarGridSpec(
            num_scalar_prefetch=2,
            grid=(DIFF_HEADS, n),
            in_specs=[pl.BlockSpec(memory_space=pltpu.SMEM)],
            out_specs=pl.BlockSpec((None, None, tq, tk), lambda h, t, *_: (h, t, 0, 0)),
        ),
        out_shape=jax.ShapeDtypeStruct((DIFF_HEADS, n, tq, tk), F32),
        compiler_params=_params("parallel", "parallel"),
        name="bias_tiles",
    )(q0, k0, rel_table)


def _diff_step(q_ref, k_ref, v_ref, bias_ref, m_sc, l_sc, acc_sc):
    tq, tk = bias_ref.shape
    cw = min(tk, LANES)
    kb = k_ref[...].astype(BF16)
    v_ext = jnp.concatenate([v_ref[...].astype(BF16), jnp.ones((tk, LANES), BF16)], axis=1)
    live = [dict(), dict()]

    def scores(c):
        cs = slice(c * HEAD_DIM, (c + 1) * HEAD_DIM)
        live[c]["s"] = _dot_nt(q_ref[:, cs], kb[:, cs]) * SCALE + bias_ref[...]

    def probs(c):
        s = live[c].pop("s")
        m_prev = m_sc[c]
        m_new = jnp.maximum(m_prev, jnp.broadcast_to(jnp.max(s, axis=1, keepdims=True), (tq, LANES)))
        live[c]["alpha"] = jnp.exp(m_prev - m_new)
        m_sc[c] = m_new
        p = [jnp.exp(s[:, j:j + cw] - m_new[:, :cw]) for j in range(0, tk, cw)]
        live[c]["p"] = jnp.concatenate(p, axis=1).astype(BF16)

    def values(c):
        live[c]["pv"] = _dot(live[c].pop("p"), v_ext)

    def update(c):
        alpha = live[c].pop("alpha")
        pv = live[c].pop("pv")
        acc_sc[c] = jnp.concatenate([alpha, alpha], axis=1) * acc_sc[c] + pv[:, :DIFF_VDIM]
        l_sc[c] = alpha * l_sc[c] + pv[:, DIFF_VDIM:]

    stages = (scores, probs, values, update)
    for t in range(2 + len(stages) - 1):
        for s in reversed(range(len(stages))):
            if 0 <= t - s < 2:
                stages[s](t - s)


def _diff_kernel(qi_ref, kk_ref, fin_ref, *refs, has_new, lam_init):
    if has_new:
        lam_ref, g_ref, q_ref, k_ref, v_ref, bias_ref, kn_ref, vn_ref, biasn_ref, o_ref, m_sc, l_sc, acc_sc = refs
    else:
        lam_ref, g_ref, q_ref, k_ref, v_ref, bias_ref, o_ref, m_sc, l_sc, acc_sc = refs
    t = pl.program_id(2)
    kk = kk_ref[t]

    @pl.when(kk == 0)
    def _():
        m_sc[...] = jnp.full(m_sc.shape, NEG, F32)
        l_sc[...] = jnp.zeros(l_sc.shape, F32)
        acc_sc[...] = jnp.zeros(acc_sc.shape, F32)

    if has_new:
        @pl.when(kk == 0)
        def _():
            _diff_step(q_ref, kn_ref, vn_ref, biasn_ref, m_sc, l_sc, acc_sc)

        @pl.when(kk > 0)
        def _():
            _diff_step(q_ref, k_ref, v_ref, bias_ref, m_sc, l_sc, acc_sc)
    else:
        _diff_step(q_ref, k_ref, v_ref, bias_ref, m_sc, l_sc, acc_sc)

    @pl.when(fin_ref[t] == 1)
    def _():
        lp = lam_ref[...]
        lam = (jnp.exp(jnp.sum(lp[0:1] * lp[1:2], axis=1, keepdims=True))
               - jnp.exp(jnp.sum(lp[2:3] * lp[3:4], axis=1, keepdims=True)) + lam_init)
        l0 = jnp.concatenate([l_sc[0], l_sc[0]], axis=1)
        l1 = jnp.concatenate([l_sc[1], l_sc[1]], axis=1)
        o = acc_sc[0] / l0 - lam * (acc_sc[1] / l1)
        o = o * lax.rsqrt(jnp.mean(o * o, axis=1, keepdims=True) + LN_EPS) * g_ref[...]
        o_ref[...] = (o * (1.0 - lam_init)).astype(o_ref.dtype)


def _attention_steps(nq, nk, causal):
    steps = [(qi, kk) for qi in range(nq) for kk in range(qi + 1 if causal else nk)]
    qi_t = jnp.array([s[0] for s in steps], jnp.int32)
    kk_t = jnp.array([s[1] for s in steps], jnp.int32)
    fin_t = jnp.array([int(kk == (qi if causal else nk - 1)) for qi, kk in steps], jnp.int32)
    return qi_t, kk_t, fin_t


def _diff_attention(q, k, v, bias, lam_vec, subln_g, layer, tq, tk, new=None):
    B, Tq, _ = q.shape
    lam_init = 0.8 - 0.6 * math.exp(-0.3 * layer)
    has_new = new is not None
    if has_new:
        nkc = k.shape[2] // tk
        tables = _attention_steps(Tq // tq, nkc + 1, causal=False)
        kv_spec = pl.BlockSpec((None, None, tk, DIFF_VDIM),
                               lambda b, h, t, qi, kk, fin: (layer, b, jnp.clip(nkc - kk[t], 0, nkc - 1), h))
        bias_spec = pl.BlockSpec((None, None, tq, tk),
                                 lambda b, h, t, qi, kk, fin: (h, jnp.clip(nkc - kk[t], 0, nkc - 1), 0, 0))
        tn = new[0].shape[1]
        new_specs = [pl.BlockSpec((None, tn, DIFF_VDIM), lambda b, h, t, qi, kk, fin: (b, 0, h)),
                     pl.BlockSpec((None, tn, DIFF_VDIM), lambda b, h, t, qi, kk, fin: (b, 0, h)),
                     pl.BlockSpec((None, None, tq, tn), lambda b, h, t, qi, kk, fin: (h, 0, 0, 0))]
        new_args = list(new)
    else:
        tables = _attention_steps(Tq // tq, k.shape[1] // tk, causal=True)
        kv_spec = pl.BlockSpec((None, tk, DIFF_VDIM), lambda b, h, t, qi, kk, fin: (b, qi[t] - kk[t], h))
        bias_spec = pl.BlockSpec((None, None, tq, tk), lambda b, h, t, qi, kk, fin: (h, kk[t], 0, 0))
        new_specs, new_args = [], []
    return pl.pallas_call(
        functools.partial(_diff_kernel, has_new=has_new, lam_init=lam_init),
        grid_spec=pltpu.PrefetchScalarGridSpec(
            num_scalar_prefetch=3,
            grid=(B, DIFF_HEADS, tables[0].shape[0]),
            in_specs=[
                pl.BlockSpec((4, HEAD_DIM), lambda b, h, t, qi, kk, fin: (0, 0)),
                pl.BlockSpec((1, DIFF_VDIM), lambda b, h, t, qi, kk, fin: (0, 0)),
                pl.BlockSpec((None, tq, DIFF_VDIM), lambda b, h, t, qi, kk, fin: (b, qi[t], h)),
                kv_spec, kv_spec, bias_spec, *new_specs,
            ],
            out_specs=pl.BlockSpec((None, tq, DIFF_VDIM), lambda b, h, t, qi, kk, fin: (b, qi[t], h)),
            scratch_shapes=[pltpu.VMEM((2, tq, LANES), F32), pltpu.VMEM((2, tq, LANES), F32),
                            pltpu.VMEM((2, tq, DIFF_VDIM), F32)],
        ),
        out_shape=jax.ShapeDtypeStruct((B, Tq, SELF_WIDTH), BF16),
        compiler_params=_params("parallel", "parallel", "arbitrary"),
        name="diff_attention",
    )(*tables, lam_vec, subln_g, q, k, v, bias, *new_args)


def _sb_step(q_ref, k_ref, v_ref, q_pos0, k_pos0, acc_sc, r_sc, cw, masked):
    tq = q_ref.shape[0]
    tk = k_ref.shape[0]
    cw = min(cw, tk)
    later = (lax.broadcasted_iota(jnp.int32, (cw, cw), 0) > lax.broadcasted_iota(jnp.int32, (cw, cw), 1))
    later = jnp.where(later, 1.0, 0.0).astype(BF16)
    units = [(c0, hh) for c0 in range(tk - cw, -1, -cw) for hh in range(SB_PAIR)]
    live = [dict() for _ in units]

    def scores(u):
        c0, hh = units[u]
        hs = slice(hh * HEAD_DIM, (hh + 1) * HEAD_DIM)
        live[u]["z"] = _dot_nt(q_ref[:, hs], k_ref[c0:c0 + cw, hs].astype(BF16)) * SCALE

    def logs(u):
        c0, _ = units[u]
        z = live[u].pop("z")
        log_keep = -(jnp.maximum(z, 0.0) + jnp.log(1.0 + jnp.exp(-jnp.abs(z))))
        live[u]["log_beta"] = log_keep + z
        if masked:
            q_pos = q_pos0 + lax.broadcasted_iota(jnp.int32, (tq, cw), 0)
            before = (k_pos0 + c0 + lax.broadcasted_iota(jnp.int32, (tq, cw), 1)) < q_pos
            log_keep = jnp.where(before, log_keep, 0.0)
            live[u]["before"] = before
        live[u]["log_keep"] = log_keep

    def cumsum(u):
        live[u]["cum"] = _dot(live[u]["log_keep"].astype(BF16), later)

    def weights(u):
        _, hh = units[u]
        a = jnp.exp(live[u].pop("log_beta") + (live[u].pop("cum") + r_sc[hh]))
        if masked:
            a = jnp.where(live[u].pop("before"), a, 0.0)
        live[u]["a"] = a.astype(BF16)
        r_sc[hh] += jnp.sum(live[u].pop("log_keep"), axis=1, keepdims=True)

    def values(u):
        c0, hh = units[u]
        hs = slice(hh * HEAD_DIM, (hh + 1) * HEAD_DIM)
        acc_sc[hh] += _dot(live[u].pop("a"), v_ref[c0:c0 + cw, hs].astype(BF16))

    stages = (scores, logs, cumsum, weights, values)
    for t in range(len(units) + len(stages) - 1):
        for s in reversed(range(len(stages))):
            if 0 <= t - s < len(units):
                stages[s](t - s)


def _sb_kernel(qi_ref, kk_ref, fin_ref, *refs, has_new, tq, tk, q_off, nkc, cw):
    if has_new:
        q_ref, k_ref, v_ref, kn_ref, vn_ref, o_ref, acc_sc, r_sc = refs
    else:
        q_ref, k_ref, v_ref, o_ref, acc_sc, r_sc = refs
    t = pl.program_id(2)
    qi = qi_ref[t]
    kk = kk_ref[t]

    @pl.when(kk == 0)
    def _():
        acc_sc[...] = jnp.zeros(acc_sc.shape, F32)
        r_sc[...] = jnp.zeros(r_sc.shape, F32)

    if has_new:
        @pl.when(kk == 0)
        def _():
            _sb_step(q_ref, kn_ref, vn_ref, q_off, q_off, acc_sc, r_sc, cw, True)

        @pl.when(kk > 0)
        def _():
            _sb_step(q_ref, k_ref, v_ref, q_off, (nkc - kk) * tk, acc_sc, r_sc, cw, False)
    else:
        @pl.when(kk == 0)
        def _():
            _sb_step(q_ref, k_ref, v_ref, qi * tq, qi * tk, acc_sc, r_sc, cw, True)

        @pl.when(kk > 0)
        def _():
            _sb_step(q_ref, k_ref, v_ref, qi * tq, (qi - kk) * tk, acc_sc, r_sc, cw, False)

    @pl.when(fin_ref[t] == 1)
    def _():
        for hh in range(SB_PAIR):
            o_ref[:, hh * HEAD_DIM:(hh + 1) * HEAD_DIM] = acc_sc[hh].astype(o_ref.dtype)


def _sb_attention(q, k, v, layer, tq, tk, new=None):
    B, Tq, _ = q.shape
    has_new = new is not None
    hw = SB_PAIR * HEAD_DIM
    if has_new:
        nkc = k.shape[2] // tk
        q_off = k.shape[2]
        tables = _attention_steps(Tq // tq, nkc + 1, causal=False)
        kv_spec = pl.BlockSpec((None, None, tk, hw),
                               lambda b, h, t, qi, kk, fin: (layer, b, jnp.clip(nkc - kk[t], 0, nkc - 1), h))
        tn = new[0].shape[1]
        new_specs = [pl.BlockSpec((None, tn, hw), lambda b, h, t, qi, kk, fin: (b, 0, h))] * 2
        new_args = list(new)
    else:
        assert tq == tk
        nkc = k.shape[1] // tk
        q_off = 0
        tables = _attention_steps(Tq // tq, nkc, causal=True)
        kv_spec = pl.BlockSpec((None, tk, hw), lambda b, h, t, qi, kk, fin: (b, qi[t] - kk[t], h))
        new_specs, new_args = [], []
    return pl.pallas_call(
        functools.partial(_sb_kernel, has_new=has_new, tq=tq, tk=tk, q_off=q_off, nkc=nkc, cw=MXU_TILE),
        grid_spec=pltpu.PrefetchScalarGridSpec(
            num_scalar_prefetch=3,
            grid=(B, SB_HEADS // SB_PAIR, tables[0].shape[0]),
            in_specs=[pl.BlockSpec((None, tq, hw), lambda b, h, t, qi, kk, fin: (b, qi[t], h)),
                      kv_spec, kv_spec, *new_specs],
            out_specs=pl.BlockSpec((None, tq, hw), lambda b, h, t, qi, kk, fin: (b, qi[t], h)),
            scratch_shapes=[pltpu.VMEM((SB_PAIR, tq, HEAD_DIM), F32), pltpu.VMEM((SB_PAIR, tq, 1), F32)],
        ),
        out_shape=jax.ShapeDtypeStruct((B, Tq, SELF_WIDTH), BF16),
        compiler_params=_params("parallel", "parallel", "arbitrary"),
        name="sb_attention",
    )(*tables, q, k, v, *new_args)


def _mem_kernel(q_ref, k_ref, v_ref, o_ref):
    q = q_ref[...]
    kb = k_ref[...].astype(BF16)
    vb = v_ref[...].astype(BF16)
    for h in range(MEM_HEADS):
        sl = slice(h * HEAD_DIM, (h + 1) * HEAD_DIM)
        s = _dot_nt(q[:, sl], kb[:, sl]) * SCALE
        p = jnp.exp(s - jnp.max(s, axis=1, keepdims=True))
        o = _dot(p.astype(BF16), vb[:, sl]) / jnp.sum(p, axis=1, keepdims=True)
        o_ref[:, sl] = o.astype(o_ref.dtype)


def _mem_attention(mq, mk, mv, tq, layer=None):
    B, Tq, _ = mq.shape
    n_mem = mk.shape[-2]
    if layer is None:
        kv_spec = pl.BlockSpec((None, n_mem, MEM_WIDTH), lambda b, qi: (b, 0, 0))
    else:
        kv_spec = pl.BlockSpec((None, None, n_mem, MEM_WIDTH), lambda b, qi: (layer, b, 0, 0))
    return pl.pallas_call(
        _mem_kernel,
        grid=(B, Tq // tq),
        in_specs=[pl.BlockSpec((None, tq, MEM_WIDTH), lambda b, qi: (b, qi, 0)), kv_spec, kv_spec],
        out_specs=pl.BlockSpec((None, tq, MEM_WIDTH), lambda b, qi: (b, qi, 0)),
        out_shape=jax.ShapeDtypeStruct((B, Tq, MEM_WIDTH), BF16),
        compiler_params=_params("parallel", "parallel"),
        name="mem_attention",
    )(mq, mk, mv)


def _top_values(x, n):
    out = []
    for _ in range(n):
        m = jnp.max(x, axis=0, keepdims=True)
        out.append(m)
        x = jnp.where(x == m, -jnp.inf, x)
    return out


def _batcher_pairs(n):
    pairs = []
    p = 1
    while p < n:
        k = p
        while k >= 1:
            for j in range(k % p, n - k, 2 * k):
                for i in range(min(k, n - j - k)):
                    if (i + j) // (2 * p) == (i + j + k) // (2 * p):
                        pairs.append((i + j, i + j + k))
            k //= 2
        p *= 2
    return pairs


def _top_values_sorted(x, n):
    g = x.shape[0] // 8
    cols = []
    for lg in range(x.shape[1] // LANES):
        v = [x[r * 8:(r + 1) * 8, lg * LANES:(lg + 1) * LANES] for r in range(g)]
        for i, j in _batcher_pairs(g):
            v[i], v[j] = jnp.maximum(v[i], v[j]), jnp.minimum(v[i], v[j])
        v.append(jnp.full_like(v[0], -jnp.inf))
        out = []
        for k in range(n):
            m = jnp.max(v[0], axis=0, keepdims=True)
            out.append(m)
            popped = v[0] == m
            for d in range(min(n - 1 - k, g)):
                v[d] = jnp.where(popped, v[d + 1], v[d])
        cols.append(out)
    return [jnp.concatenate([c[k] for c in cols], axis=1) for k in range(n)]


def _peer_select_kernel(q_ref, keys_ref, thr_ref, e1z_ref, s2_ref, e2_ref, cand_sc):
    n_top = PEER_TOPK + 1
    pairs = [(p, q) for p in range(n_top) for q in range(n_top) if (p + 1) * (q + 1) <= n_top]
    for h in range(PEER_HEADS):
        s = []
        for c in range(2):
            col = (2 * h + c) * HEAD_DIM
            s.append(_dot_nt(keys_ref[h, c], q_ref[:, col:col + HEAD_DIM]))
        a = _top_values_sorted(s[0], n_top)
        b = _top_values_sorted(s[1], n_top)
        cand_sc[...] = jnp.full(cand_sc.shape, -jnp.inf, F32)
        for r, (p, q) in enumerate(pairs):
            cand_sc[r:r + 1, :] = a[p] + b[q]
        c = _top_values_sorted(cand_sc[...], n_top)
        z = jnp.ones_like(c[0])
        for kth in range(1, PEER_TOPK):
            z = z + jnp.exp(c[kth] - c[0])
        tau = 0.5 * (c[PEER_TOPK - 1] + c[PEER_TOPK])
        thr_ref[h] = tau - s[0]
        e1z_ref[h] = jnp.exp(s[0] - a[0]) / z
        e2 = jnp.exp(s[1] - b[0])
        for lg in range(s2_ref.shape[1]):
            s2_ref[h, lg] = s[1][:, lg * LANES:(lg + 1) * LANES]
            e2_ref[h, lg] = e2[:, lg * LANES:(lg + 1) * LANES]


def _peer_select(qp, keys, layer, tt):
    T = qp.shape[0]
    tt = _tile(T, tt)
    n_pairs = sum(1 for p in range(1, PEER_TOPK + 2) for q in range(1, PEER_TOPK + 2) if p * q <= PEER_TOPK + 1)
    out = jax.ShapeDtypeStruct((PEER_HEADS, N_KEYS, T), F32)
    spec = pl.BlockSpec((PEER_HEADS, N_KEYS, tt), lambda i: (0, 0, i))
    out_lg = jax.ShapeDtypeStruct((PEER_HEADS, T // LANES, N_KEYS, LANES), F32)
    spec_lg = pl.BlockSpec((PEER_HEADS, tt // LANES, N_KEYS, LANES), lambda i: (0, i, 0, 0))
    return pl.pallas_call(
        _peer_select_kernel,
        grid=(T // tt,),
        in_specs=[pl.BlockSpec((tt, PEER_HEADS * 2 * HEAD_DIM), lambda i: (i, 0)),
                  pl.BlockSpec((None, PEER_HEADS, 2, N_KEYS, HEAD_DIM), lambda i: (layer, 0, 0, 0, 0))],
        out_specs=[spec, spec, spec_lg, spec_lg],
        out_shape=[out, out, out_lg, out_lg],
        scratch_shapes=[pltpu.VMEM((8 * pl.next_power_of_2(-(-n_pairs // 8)), tt), F32)],
        compiler_params=_params("parallel"),
        name="peer_select",
    )(qp, keys)


def _peer_gate(thr_ref, e1z_ref, s2_ref, e2_ref, act_ref, hid_ref, rows, row0, slab, lg, sb):
    ls = slice(lg * LANES, (lg + 1) * LANES)
    js = slice(sb * slab, (sb + 1) * slab)
    gates = [None] * rows
    for h in range(PEER_HEADS):
        s2 = s2_ref[h, lg, js, :]
        e2 = e2_ref[h, lg, js, :]
        for ii in range(rows):
            thr = jnp.broadcast_to(thr_ref[h, row0 + ii:row0 + ii + 1, ls], (slab, LANES))
            e1z = jnp.broadcast_to(e1z_ref[h, row0 + ii:row0 + ii + 1, ls], (slab, LANES))
            g = jnp.where(s2 >= thr, e2 * e1z, 0.0)
            gates[ii] = g if gates[ii] is None else gates[ii] + g
    for ii in range(rows):
        es = slice(ii * N_KEYS + sb * slab, ii * N_KEYS + (sb + 1) * slab)
        a = act_ref[es, ls]
        hid_ref[es, ls] = (0.5 * a * (1.0 + lax.erf(a * (2.0 ** -0.5))) * gates[ii]).astype(BF16)


def _peer_mix_kernel(xt_ref, u_ref, vt_ref, thr_ref, e1z_ref, s2_ref, e2_ref, h_ref, g_ref, b_ref, *refs,
                     rows, ne, alpha, n_ride):
    ride_refs = refs[:n_ride]
    y_ref = refs[n_ride]
    stack_refs = refs[n_ride + 1:-5]
    acc_sc, act0, act1, hid0, hid1 = refs[-5:]
    for j, src in enumerate(ride_refs):
        per = n_ride // len(stack_refs)
        stack_refs[j // per][j % per] = src[...]

    f = pl.program_id(0)
    tt = xt_ref.shape[1]
    slab = GATE_ACC_VREGS * 8 // rows

    @pl.when(f == 0)
    def _():
        acc_sc[...] = jnp.zeros(acc_sc.shape, F32)
        for r in (act0, act1, hid0, hid1):
            r[...] = jnp.zeros(r.shape, r.dtype)

    def stages(act_w, act_r, hid_w, hid_r, row0):
        mt = MXU_TILE
        pieces = []
        m1 = mt
        kp = PEER_MM_TILES * mt
        for n in range(tt // mt):
            for m in range(act_w.shape[0] // m1):
                for k in range(xt_ref.shape[0] // kp):
                    def mm1(m=m, n=n, k=k):
                        d = _dot(u_ref[m * m1:(m + 1) * m1, k * kp:(k + 1) * kp],
                                 xt_ref[k * kp:(k + 1) * kp, n * mt:(n + 1) * mt])
                        if k == 0:
                            act_w[m * m1:(m + 1) * m1, n * mt:(n + 1) * mt] = d
                        else:
                            act_w[m * m1:(m + 1) * m1, n * mt:(n + 1) * mt] += d
                    pieces.append((mm1, PEER_MM_TILES))
        m2 = mt
        mn = min(tt, (PEER_MM_TILES * mt * mt) // hid_r.shape[0])
        for m in range(acc_sc.shape[0] // m2):
            for n in range(tt // mn):
                def mm2(m=m, n=n):
                    acc_sc[m * m2:(m + 1) * m2, n * mn:(n + 1) * mn] += _dot(
                        vt_ref[m * m2:(m + 1) * m2, :], hid_r[:, n * mn:(n + 1) * mn])
                pieces.append((mm2, (hid_r.shape[0] // mt) * (mn // mt)))
        groups = [(lg, sb) for lg in range(tt // LANES) for sb in range(N_KEYS // slab)]
        total = sum(c for _, c in pieces)
        issued = 0
        for gi, (lg, sb) in enumerate(groups):
            while pieces and issued * len(groups) <= gi * total:
                fn, cost = pieces.pop(0)
                fn()
                issued += cost
            _peer_gate(thr_ref, e1z_ref, s2_ref, e2_ref, act_r, hid_w, rows, row0, slab, lg, sb)
        for fn, _ in pieces:
            fn()

    @pl.when(f % 2 == 0)
    def _():
        stages(act0, act1, hid1, hid0, rows % 8)

    @pl.when(f % 2 == 1)
    def _():
        stages(act1, act0, hid0, hid1, 0)

    @pl.when((f >= 2) & ((f - 2) % ne == ne - 1))
    def _():
        y_ref[...] = _layer_norm(alpha * h_ref[...] + acc_sc[...].T, g_ref[...], b_ref[...])
        acc_sc[...] = jnp.zeros(acc_sc.shape, F32)


def _peer_mix(xt, u, vt, sel, layer, h, ln_g, ln_b, alpha, tt, rows, stack=()):
    D, T = xt.shape
    tt = _tile(T, tt)
    ne, eb = vt.shape[1], vt.shape[3]
    assert eb == rows * N_KEYS and ne * eb == u.shape[1] and rows in (4, 8) and ne % 2 == 0
    last = (T // tt) * ne - 1

    def tile_of(f):
        return jnp.clip(f, 0, last) // ne

    def block_of(f):
        return jnp.clip(f, 0, last) % ne

    row_spec = pl.BlockSpec((PEER_HEADS, None, 8, tt), lambda f: (0, block_of(f - 1) * rows // 8, 0, tile_of(f - 1)))
    full_spec = pl.BlockSpec((PEER_HEADS, tt // LANES, N_KEYS, LANES), lambda f: (0, tile_of(f - 1), 0, 0))
    thr, e1z, s2, e2 = sel
    thr = thr.reshape(PEER_HEADS, N_KEYS // 8, 8, T)
    e1z = e1z.reshape(PEER_HEADS, N_KEYS // 8, 8, T)
    ride = [a for group in stack for a in group]
    rb = T // (last + 1)
    assert not ride or (T % (last + 1) == 0 and rb % 8 == 0 and len({len(g) for g in stack}) == 1)
    ride_specs = [pl.BlockSpec((rb, a.shape[1]), lambda f: (jnp.clip(f, 0, last), 0)) for a in ride]
    stack_specs = [pl.BlockSpec((len(g), rb, g[0].shape[1]), lambda f: (0, jnp.clip(f, 0, last), 0)) for g in stack]
    stack_shapes = [jax.ShapeDtypeStruct((len(g), T, g[0].shape[1]), g[0].dtype) for g in stack]
    return pl.pallas_call(
        functools.partial(_peer_mix_kernel, rows=rows, ne=ne, alpha=alpha, n_ride=len(ride)),
        grid=(last + 3,),
        in_specs=[
            pl.BlockSpec((D, tt), lambda f: (0, tile_of(f))),
            pl.BlockSpec((None, eb, D), lambda f: (layer, block_of(f), 0)),
            pl.BlockSpec((None, None, D, eb), lambda f: (layer, block_of(f - 2), 0, 0)),
            row_spec, row_spec, full_spec, full_spec,
            pl.BlockSpec((tt, D), lambda f: (tile_of(f - 2), 0), pipeline_mode=pl.Buffered(1)),
            pl.BlockSpec((None, 1, D), lambda f: (layer, 0, 0)),
            pl.BlockSpec((None, 1, D), lambda f: (layer, 0, 0)),
            *ride_specs,
        ],
        out_specs=[pl.BlockSpec((tt, D), lambda f: (tile_of(f - 2), 0)), *stack_specs],
        out_shape=[jax.ShapeDtypeStruct((T, D), F32), *stack_shapes],
        scratch_shapes=[pltpu.VMEM((D, tt), F32), pltpu.VMEM((eb, tt), F32), pltpu.VMEM((eb, tt), F32),
                        pltpu.VMEM((eb, tt), BF16), pltpu.VMEM((eb, tt), BF16)],
        compiler_params=_params("arbitrary"),
        name="peer_mix",
    )(xt, u, vt, thr, e1z, s2, e2, h, ln_g, ln_b, *ride)


def kernel(x_prompt, x_sample, cache_self_k, cache_self_v, cache_mem_k, cache_mem_v, mem_prompt, w_in, w_o, w_mem_k, w_mem_v, rel_bias_table, diff_lambda, diff_subln_g, ln_g, ln_b, peer_w_q, peer_sub_keys, peer_u, peer_v):
    depth = w_in.shape[0]
    bp, n_prompt, d_model = x_prompt.shape
    bs, n_new, _ = x_sample.shape
    n_past = cache_self_k.shape[2]
    n_mem = mem_prompt.shape[1]
    alpha = (2 * depth) ** 0.25
    S = SELF_WIDTH

    w_in_b = w_in.astype(BF16)
    w_o_b = w_o.astype(BF16)
    w_mk_b = w_mem_k.astype(BF16)
    w_mv_b = w_mem_v.astype(BF16)
    w_pq_b = peer_w_q.astype(BF16)
    keys_b = peer_sub_keys.astype(BF16)
    u_b = peer_u.astype(BF16)
    eb = PEER_ROWS * N_KEYS
    vt_b = jnp.swapaxes(peer_v.reshape(depth, -1, eb, d_model), 2, 3).astype(BF16)
    ln_g4 = ln_g.reshape(depth, 2, 1, d_model)
    ln_b4 = ln_b.reshape(depth, 2, 1, d_model)
    cmk = cache_mem_k.reshape(depth, bs, n_mem, MEM_WIDTH)
    cmv = cache_mem_v.reshape(depth, bs, n_mem, MEM_WIDTH)
    mem2d = mem_prompt.reshape(bp * n_mem, d_model)

    tq_p = _tile(n_prompt, 512)
    tk_s = _tile(n_past, CACHE_TILE)
    nq_p = n_prompt // tq_p
    nk_s = n_past // tk_s
    i32 = jnp.int32
    bias_p = _bias_tiles(rel_bias_table, jnp.arange(nq_p, dtype=i32) * tq_p, jnp.zeros((nq_p,), i32), tq_p, tq_p)
    bias_sc = _bias_tiles(rel_bias_table, jnp.full((nk_s,), n_past, i32), jnp.arange(nk_s, dtype=i32) * tk_s, n_new, tk_s)
    bias_sn = _bias_tiles(rel_bias_table, jnp.full((1,), n_past, i32), jnp.full((1,), n_past, i32), n_new, n_new)

    hp = x_prompt.reshape(bp * n_prompt, d_model)
    hs = x_sample.reshape(bs * n_new, d_model)
    segs = ((0, S), (S, S), (2 * S, S), (3 * S, MEM_WIDTH))
    seg_dt = (BF16, F32, F32, BF16)
    new_k_p, new_v_p, new_mk_p, new_mv_p, new_k_s, new_v_s = [], [], [], [], [], []
    for i in range(depth):
        kind = i % 2
        j = i // 2
        qp, kp, vp, mqp = _linear(hp, w_in_b, i, segs, seg_dt, ROW_TILE)
        qs, ks, vs, mqs = _linear(hs, w_in_b, i, segs, seg_dt, ROW_TILE)
        (mkp,) = _linear(mem2d, w_mk_b, i, ((0, MEM_WIDTH),), (F32,), ROW_TILE)
        (mvp,) = _linear(mem2d, w_mv_b, i, ((0, MEM_WIDTH),), (F32,), ROW_TILE)
        qp3, kp3, vp3 = (a.reshape(bp, n_prompt, S) for a in (qp, kp, vp))
        qs3, ks3, vs3 = (a.reshape(bs, n_new, S) for a in (qs, ks, vs))

        if kind == 0:
            g = diff_subln_g[j].reshape(1, DIFF_VDIM)
            op = _diff_attention(qp3, kp3, vp3, bias_p, diff_lambda[j], g, i, tq_p, tq_p)
            os_ = _diff_attention(qs3, cache_self_k, cache_self_v, bias_sc, diff_lambda[j], g, i, n_new, tk_s,
                                  new=(ks3, vs3, bias_sn))
        else:
            op = _sb_attention(qp3, kp3, vp3, i, tq_p, tq_p)
            os_ = _sb_attention(qs3, cache_self_k, cache_self_v, i, n_new, tk_s, new=(ks3, vs3))

        mop = _mem_attention(mqp.reshape(bp, n_prompt, MEM_WIDTH), mkp.reshape(bp, n_mem, MEM_WIDTH),
                             mvp.reshape(bp, n_mem, MEM_WIDTH), tq_p)
        mos = _mem_attention(mqs.reshape(bs, n_new, MEM_WIDTH), cmk, cmv, n_new, layer=i)

        new_k_p.append(kp)
        new_v_p.append(vp)
        outs = []
        for h_res, o_self, o_mem, is_prompt in ((hp, op, mop, True), (hs, os_, mos, False)):
            T = h_res.shape[0]
            y, yt = _out_proj(o_self.reshape(T, S), o_mem.reshape(T, MEM_WIDTH), w_o_b, i, h_res,
                              ln_g4[:, 0], ln_b4[:, 0], alpha, ROW_TILE)
            (pq,) = _linear(y, w_pq_b, i, ((0, PEER_HEADS * 2 * HEAD_DIM),), (BF16,), ROW_TILE)
            sel = _peer_select(pq, keys_b, i, ROW_TILE)
            stack = (new_k_p, new_v_p) if is_prompt and i == depth - 1 else ()
            y2, *stacked = _peer_mix(yt, u_b, vt_b, sel, i, y, ln_g4[:, 1], ln_b4[:, 1], alpha, ROW_TILE, PEER_ROWS,
                                     stack=stack)
            outs.append(y2)
            if stack:
                k_p_all, v_p_all = (a.reshape(depth, bp, n_prompt, S) for a in stacked)
        hp, hs = outs

        new_mk_p.append(mkp.reshape(bp, n_mem, MEM_HEADS, HEAD_DIM))
        new_mv_p.append(mvp.reshape(bp, n_mem, MEM_HEADS, HEAD_DIM))
        new_k_s.append(ks3)
        new_v_s.append(vs3)

    return (hp.reshape(bp, n_prompt, d_model), hs.reshape(bs, n_new, d_model),
            k_p_all, v_p_all, jnp.stack(new_mk_p), jnp.stack(new_mv_p),
            jnp.stack(new_k_s), jnp.stack(new_v_s))
```

```python
import functools
import math

import jax
import jax.numpy as jnp
from jax import lax
from jax.experimental import pallas as pl
from jax.experimental.pallas import tpu as pltpu

F32 = jnp.float32
BF16 = jnp.bfloat16

HEAD_DIM = 128
SELF_WIDTH = 1536
DIFF_HEADS = 6
DIFF_VDIM = 2 * HEAD_DIM
SB_HEADS = 12
MEM_HEADS = 4
MEM_WIDTH = MEM_HEADS * HEAD_DIM
CHUNK = 64
N_BUCKETS = 32
PEER_HEADS = 8
N_KEYS = 128
PEER_TOPK = 16
LN_EPS = 1e-5
SCALE = HEAD_DIM ** -0.5
NEG = -1e30

V7X_VMEM_LIMIT = 56 * 1024 * 1024
LANES = 128
MXU_TILE = 256
ROW_TILE = 512
CACHE_TILE = 2048
SB_PAIR = 4
PEER_MM_TILES = 4
PEER_ROWS = 4
GATE_ACC_VREGS = 16


def _params(*sem):
    return pltpu.CompilerParams(dimension_semantics=sem, vmem_limit_bytes=V7X_VMEM_LIMIT)


def _tile(n, t):
    t = min(n, t)
    assert n % t == 0, (n, t)
    return t


def _dot(a, b):
    return jnp.dot(a, b, preferred_element_type=F32)


def _dot_nt(a, b):
    return lax.dot_general(a, b, (((1,), (1,)), ((), ())), preferred_element_type=F32)


def _layer_norm(r, g, b):
    mu = jnp.mean(r, axis=-1, keepdims=True)
    d = r - mu
    var = jnp.mean(d * d, axis=-1, keepdims=True)
    return d * lax.rsqrt(var + LN_EPS) * g + b


def _linear_kernel(x_ref, w_ref, *o_refs, segs, nc):
    x = x_ref[...].astype(BF16)
    for o_ref, (start, width) in zip(o_refs, segs):
        for c in range(0, width, nc):
            o_ref[:, c:c + nc] = _dot(x, w_ref[:, start + c:start + c + nc]).astype(o_ref.dtype)


def _linear(x, w, layer, segs, dtypes, tm):
    M, K = x.shape
    N = w.shape[2]
    tm = _tile(M, tm)
    nc = 512
    assert all(wd % nc == 0 for _, wd in segs)
    return pl.pallas_call(
        functools.partial(_linear_kernel, segs=segs, nc=nc),
        grid=(M // tm,),
        in_specs=[
            pl.BlockSpec((tm, K), lambda i: (i, 0)),
            pl.BlockSpec((None, K, N), lambda i: (layer, 0, 0), pipeline_mode=pl.Buffered(1)),
        ],
        out_specs=[pl.BlockSpec((tm, wd), lambda i: (i, 0)) for _, wd in segs],
        out_shape=[jax.ShapeDtypeStruct((M, wd), dt) for (_, wd), dt in zip(segs, dtypes)],
        compiler_params=_params("parallel"),
        name="linear",
    )(x, w)


def _out_proj_kernel(o_ref, mo_ref, w_ref, h_ref, g_ref, b_ref, y_ref, yt_ref, *, alpha):
    acc = _dot(o_ref[...], w_ref[:SELF_WIDTH, :]) + _dot(mo_ref[...], w_ref[SELF_WIDTH:, :])
    y = _layer_norm(alpha * h_ref[...] + acc, g_ref[...], b_ref[...])
    y_ref[...] = y
    yt_ref[...] = y.T.astype(BF16)


def _out_proj(o, mo, w_o, layer, h, ln_g, ln_b, alpha, tm):
    T, D = h.shape
    tm = _tile(T, tm)
    return pl.pallas_call(
        functools.partial(_out_proj_kernel, alpha=alpha),
        grid=(T // tm,),
        in_specs=[
            pl.BlockSpec((tm, SELF_WIDTH), lambda i: (i, 0)),
            pl.BlockSpec((tm, MEM_WIDTH), lambda i: (i, 0)),
            pl.BlockSpec((None, SELF_WIDTH + MEM_WIDTH, D), lambda i: (layer, 0, 0), pipeline_mode=pl.Buffered(1)),
            pl.BlockSpec((tm, D), lambda i: (i, 0)),
            pl.BlockSpec((None, 1, D), lambda i: (layer, 0, 0)),
            pl.BlockSpec((None, 1, D), lambda i: (layer, 0, 0)),
        ],
        out_specs=[pl.BlockSpec((tm, D), lambda i: (i, 0)), pl.BlockSpec((D, tm), lambda i: (0, i))],
        out_shape=[jax.ShapeDtypeStruct((T, D), F32), jax.ShapeDtypeStruct((D, T), BF16)],
        compiler_params=_params("parallel"),
        name="out_proj_ln",
    )(o, mo, w_o, h, ln_g, ln_b)


def _bias_kernel(q0_ref, k0_ref, tab_ref, o_ref, *, tq, tk):
    h = pl.program_id(0)
    t = pl.program_id(1)
    q_pos = q0_ref[t] + lax.broadcasted_iota(jnp.int32, (tq, tk), 0)
    k_pos = k0_ref[t] + lax.broadcasted_iota(jnp.int32, (tq, tk), 1)
    rel = k_pos - q_pos
    n = jnp.abs(rel)
    large = jnp.full((tq, tk), 8, jnp.int32)
    for thr in (12, 16, 23, 32, 46, 64, 91):
        large = large + (n >= thr).astype(jnp.int32)
    bucket = jnp.where(rel > 0, N_BUCKETS // 2, 0) + jnp.where(n < 8, n, large)
    bias = jnp.zeros((tq, tk), F32)
    for b in range(N_BUCKETS):
        bias = jnp.where(bucket == b, tab_ref[b, h], bias)
    visible = (k_pos // CHUNK) <= (q_pos // CHUNK)
    o_ref[...] = jnp.where(visible, bias, NEG)


def _bias_tiles(rel_table, q0, k0, tq, tk):
    n = q0.shape[0]
    return pl.pallas_call(
        functools.partial(_bias_kernel, tq=tq, tk=tk),
        grid_spec=pltpu.PrefetchScalarGridSpec(
            num_scalar_prefetch=2,
            grid=(DIFF_HEADS, n),
            in_specs=[pl.BlockSpec(memory_space=pltpu.SMEM)],
            out_specs=pl.BlockSpec((None, None, tq, tk), lambda h, t, *_: (h, t, 0, 0)),
        ),
        out_shape=jax.ShapeDtypeStruct((DIFF_HEADS, n, tq, tk), F32),
        compiler_params=_params("parallel", "parallel"),
        name="bias_tiles",
    )(q0, k0, rel_table)


def _diff_step(q_ref, k_ref, v_ref, bias_ref, m_sc, l_sc, acc_sc):
    tq, tk = bias_ref.shape
    cw = min(tk, LANES)
    kb = k_ref[...].astype(BF16)
    v_ext = jnp.concatenate([v_ref[...].astype(BF16), jnp.ones((tk, LANES), BF16)], axis=1)
    live = [dict(), dict()]

    def scores(c):
        cs = slice(c * HEAD_DIM, (c + 1) * HEAD_DIM)
        live[c]["s"] = _dot_nt(q_ref[:, cs], kb[:, cs]) * SCALE + bias_ref[...]

    def probs(c):
        s = live[c].pop("s")
        m_prev = m_sc[c]
        m_new = jnp.maximum(m_prev, jnp.broadcast_to(jnp.max(s, axis=1, keepdims=True), (tq, LANES)))
        live[c]["alpha"] = jnp.exp(m_prev - m_new)
        m_sc[c] = m_new
        p = [jnp.exp(s[:, j:j + cw] - m_new[:, :cw]) for j in range(0, tk, cw)]
        live[c]["p"] = jnp.concatenate(p, axis=1).astype(BF16)

    def values(c):
        live[c]["pv"] = _dot(live[c].pop("p"), v_ext)

    def update(c):
        alpha = live[c].pop("alpha")
        pv = live[c].pop("pv")
        acc_sc[c] = jnp.concatenate([alpha, alpha], axis=1) * acc_sc[c] + pv[:, :DIFF_VDIM]
        l_sc[c] = alpha * l_sc[c] + pv[:, DIFF_VDIM:]

    stages = (scores, probs, values, update)
    for t in range(2 + len(stages) - 1):
        for s in reversed(range(len(stages))):
            if 0 <= t - s < 2:
                stages[s](t - s)


def _diff_kernel(qi_ref, kk_ref, fin_ref, *refs, has_new, lam_init):
    if has_new:
        lam_ref, g_ref, q_ref, k_ref, v_ref, bias_ref, kn_ref, vn_ref, biasn_ref, o_ref, m_sc, l_sc, acc_sc = refs
    else:
        lam_ref, g_ref, q_ref, k_ref, v_ref, bias_ref, o_ref, m_sc, l_sc, acc_sc = refs
    t = pl.program_id(2)
    kk = kk_ref[t]

    @pl.when(kk == 0)
    def _():
        m_sc[...] = jnp.full(m_sc.shape, NEG, F32)
        l_sc[...] = jnp.zeros(l_sc.shape, F32)
        acc_sc[...] = jnp.zeros(acc_sc.shape, F32)

    if has_new:
        @pl.when(kk == 0)
        def _():
            _diff_step(q_ref, kn_ref, vn_ref, biasn_ref, m_sc, l_sc, acc_sc)

        @pl.when(kk > 0)
        def _():
            _diff_step(q_ref, k_ref, v_ref, bias_ref, m_sc, l_sc, acc_sc)
    else:
        _diff_step(q_ref, k_ref, v_ref, bias_ref, m_sc, l_sc, acc_sc)

    @pl.when(fin_ref[t] == 1)
    def _():
        lp = lam_ref[...]
        lam = (jnp.exp(jnp.sum(lp[0:1] * lp[1:2], axis=1, keepdims=True))
               - jnp.exp(jnp.sum(lp[2:3] * lp[3:4], axis=1, keepdims=True)) + lam_init)
        l0 = jnp.concatenate([l_sc[0], l_sc[0]], axis=1)
        l1 = jnp.concatenate([l_sc[1], l_sc[1]], axis=1)
        o = acc_sc[0] / l0 - lam * (acc_sc[1] / l1)
        o = o * lax.rsqrt(jnp.mean(o * o, axis=1, keepdims=True) + LN_EPS) * g_ref[...]
        o_ref[...] = (o * (1.0 - lam_init)).astype(o_ref.dtype)


def _attention_steps(nq, nk, causal):
    steps = [(qi, kk) for qi in range(nq) for kk in range(qi + 1 if causal else nk)]
    qi_t = jnp.array([s[0] for s in steps], jnp.int32)
    kk_t = jnp.array([s[1] for s in steps], jnp.int32)
    fin_t = jnp.array([int(kk == (qi if causal else nk - 1)) for qi, kk in steps], jnp.int32)
    return qi_t, kk_t, fin_t


def _diff_attention(q, k, v, bias, lam_vec, subln_g, layer, tq, tk, new=None):
    B, Tq, _ = q.shape
    lam_init = 0.8 - 0.6 * math.exp(-0.3 * layer)
    has_new = new is not None
    if has_new:
        nkc = k.shape[2] // tk
        tables = _attention_steps(Tq // tq, nkc + 1, causal=False)
        kv_spec = pl.BlockSpec((None, None, tk, DIFF_VDIM),
                               lambda b, h, t, qi, kk, fin: (layer, b, jnp.clip(nkc - kk[t], 0, nkc - 1), h))
        bias_spec = pl.BlockSpec((None, None, tq, tk),
                                 lambda b, h, t, qi, kk, fin: (h, jnp.clip(nkc - kk[t], 0, nkc - 1), 0, 0))
        tn = new[0].shape[1]
        new_specs = [pl.BlockSpec((None, tn, DIFF_VDIM), lambda b, h, t, qi, kk, fin: (b, 0, h)),
                     pl.BlockSpec((None, tn, DIFF_VDIM), lambda b, h, t, qi, kk, fin: (b, 0, h)),
                     pl.BlockSpec((None, None, tq, tn), lambda b, h, t, qi, kk, fin: (h, 0, 0, 0))]
        new_args = list(new)
    else:
        tables = _attention_steps(Tq // tq, k.shape[1] // tk, causal=True)
        kv_spec = pl.BlockSpec((None, tk, DIFF_VDIM), lambda b, h, t, qi, kk, fin: (b, qi[t] - kk[t], h))
        bias_spec = pl.BlockSpec((None, None, tq, tk), lambda b, h, t, qi, kk, fin: (h, kk[t], 0, 0))
        new_specs, new_args = [], []
    return pl.pallas_call(
        functools.partial(_diff_kernel, has_new=has_new, lam_init=lam_init),
        grid_spec=pltpu.PrefetchScalarGridSpec(
            num_scalar_prefetch=3,
            grid=(B, DIFF_HEADS, tables[0].shape[0]),
            in_specs=[
                pl.BlockSpec((4, HEAD_DIM), lambda b, h, t, qi, kk, fin: (0, 0)),
                pl.BlockSpec((1, DIFF_VDIM), lambda b, h, t, qi, kk, fin: (0, 0)),
                pl.BlockSpec((None, tq, DIFF_VDIM), lambda b, h, t, qi, kk, fin: (b, qi[t], h)),
                kv_spec, kv_spec, bias_spec, *new_specs,
            ],
            out_specs=pl.BlockSpec((None, tq, DIFF_VDIM), lambda b, h, t, qi, kk, fin: (b, qi[t], h)),
            scratch_shapes=[pltpu.VMEM((2, tq, LANES), F32), pltpu.VMEM((2, tq, LANES), F32),
                            pltpu.VMEM((2, tq, DIFF_VDIM), F32)],
        ),
        out_shape=jax.ShapeDtypeStruct((B, Tq, SELF_WIDTH), BF16),
        compiler_params=_params("parallel", "parallel", "arbitrary"),
        name="diff_attention",
    )(*tables, lam_vec, subln_g, q, k, v, bias, *new_args)


def _sb_step(q_ref, k_ref, v_ref, q_pos0, k_pos0, acc_sc, r_sc, cw, masked):
    tq = q_ref.shape[0]
    tk = k_ref.shape[0]
    cw = min(cw, tk)
    later = (lax.broadcasted_iota(jnp.int32, (cw, cw), 0) > lax.broadcasted_iota(jnp.int32, (cw, cw), 1))
    later = jnp.where(later, 1.0, 0.0).astype(BF16)
    units = [(c0, hh) for c0 in range(tk - cw, -1, -cw) for hh in range(SB_PAIR)]
    live = [dict() for _ in units]

    def scores(u):
        c0, hh = units[u]
        hs = slice(hh * HEAD_DIM, (hh + 1) * HEAD_DIM)
        live[u]["z"] = _dot_nt(q_ref[:, hs], k_ref[c0:c0 + cw, hs].astype(BF16)) * SCALE

    def logs(u):
        c0, _ = units[u]
        z = live[u].pop("z")
        log_keep = -(jnp.maximum(z, 0.0) + jnp.log(1.0 + jnp.exp(-jnp.abs(z))))
        live[u]["log_beta"] = log_keep + z
        if masked:
            q_pos = q_pos0 + lax.broadcasted_iota(jnp.int32, (tq, cw), 0)
            before = (k_pos0 + c0 + lax.broadcasted_iota(jnp.int32, (tq, cw), 1)) < q_pos
            log_keep = jnp.where(before, log_keep, 0.0)
            live[u]["before"] = before
        live[u]["log_keep"] = log_keep

    def cumsum(u):
        live[u]["cum"] = _dot(live[u]["log_keep"].astype(BF16), later)

    def weights(u):
        _, hh = units[u]
        a = jnp.exp(live[u].pop("log_beta") + (live[u].pop("cum") + r_sc[hh]))
        if masked:
            a = jnp.where(live[u].pop("before"), a, 0.0)
        live[u]["a"] = a.astype(BF16)
        r_sc[hh] += jnp.sum(live[u].pop("log_keep"), axis=1, keepdims=True)

    def values(u):
        c0, hh = units[u]
        hs = slice(hh * HEAD_DIM, (hh + 1) * HEAD_DIM)
        acc_sc[hh] += _dot(live[u].pop("a"), v_ref[c0:c0 + cw, hs].astype(BF16))

    stages = (scores, logs, cumsum, weights, values)
    for t in range(len(units) + len(stages) - 1):
        for s in reversed(range(len(stages))):
            if 0 <= t - s < len(units):
                stages[s](t - s)


def _sb_kernel(qi_ref, kk_ref, fin_ref, *refs, has_new, tq, tk, q_off, nkc, cw):
    if has_new:
        q_ref, k_ref, v_ref, kn_ref, vn_ref, o_ref, acc_sc, r_sc = refs
    else:
        q_ref, k_ref, v_ref, o_ref, acc_sc, r_sc = refs
    t = pl.program_id(2)
    qi = qi_ref[t]
    kk = kk_ref[t]

    @pl.when(kk == 0)
    def _():
        acc_sc[...] = jnp.zeros(acc_sc.shape, F32)
        r_sc[...] = jnp.zeros(r_sc.shape, F32)

    if has_new:
        @pl.when(kk == 0)
        def _():
            _sb_step(q_ref, kn_ref, vn_ref, q_off, q_off, acc_sc, r_sc, cw, True)

        @pl.when(kk > 0)
        def _():
            _sb_step(q_ref, k_ref, v_ref, q_off, (nkc - kk) * tk, acc_sc, r_sc, cw, False)
    else:
        @pl.when(kk == 0)
        def _():
            _sb_step(q_ref, k_ref, v_ref, qi * tq, qi * tk, acc_sc, r_sc, cw, True)

        @pl.when(kk > 0)
        def _():
            _sb_step(q_ref, k_ref, v_ref, qi * tq, (qi - kk) * tk, acc_sc, r_sc, cw, False)

    @pl.when(fin_ref[t] == 1)
    def _():
        for hh in range(SB_PAIR):
            o_ref[:, hh * HEAD_DIM:(hh + 1) * HEAD_DIM] = acc_sc[hh].astype(o_ref.dtype)


def _sb_attention(q, k, v, layer, tq, tk, new=None):
    B, Tq, _ = q.shape
    has_new = new is not None
    hw = SB_PAIR * HEAD_DIM
    if has_new:
        nkc = k.shape[2] // tk
        q_off = k.shape[2]
        tables = _attention_steps(Tq // tq, nkc + 1, causal=False)
        kv_spec = pl.BlockSpec((None, None, tk, hw),
                               lambda b, h, t, qi, kk, fin: (layer, b, jnp.clip(nkc - kk[t], 0, nkc - 1), h))
        tn = new[0].shape[1]
        new_specs = [pl.BlockSpec((None, tn, hw), lambda b, h, t, qi, kk, fin: (b, 0, h))] * 2
        new_args = list(new)
    else:
        assert tq == tk
        nkc = k.shape[1] // tk
        q_off = 0
        tables = _attention_steps(Tq // tq, nkc, causal=True)
        kv_spec = pl.BlockSpec((None, tk, hw), lambda b, h, t, qi, kk, fin: (b, qi[t] - kk[t], h))
        new_specs, new_args = [], []
    return pl.pallas_call(
        functools.partial(_sb_kernel, has_new=has_new, tq=tq, tk=tk, q_off=q_off, nkc=nkc, cw=MXU_TILE),
        grid_spec=pltpu.PrefetchScalarGridSpec(
            num_scalar_prefetch=3,
            grid=(B, SB_HEADS // SB_PAIR, tables[0].shape[0]),
            in_specs=[pl.BlockSpec((None, tq, hw), lambda b, h, t, qi, kk, fin: (b, qi[t], h)),
                      kv_spec, kv_spec, *new_specs],
            out_specs=pl.BlockSpec((None, tq, hw), lambda b, h, t, qi, kk, fin: (b, qi[t], h)),
            scratch_shapes=[pltpu.VMEM((SB_PAIR, tq, HEAD_DIM), F32), pltpu.VMEM((SB_PAIR, tq, 1), F32)],
        ),
        out_shape=jax.ShapeDtypeStruct((B, Tq, SELF_WIDTH), BF16),
        compiler_params=_params("parallel", "parallel", "arbitrary"),
        name="sb_attention",
    )(*tables, q, k, v, *new_args)


def _mem_kernel(q_ref, k_ref, v_ref, o_ref):
    q = q_ref[...]
    kb = k_ref[...].astype(BF16)
    vb = v_ref[...].astype(BF16)
    for h in range(MEM_HEADS):
        sl = slice(h * HEAD_DIM, (h + 1) * HEAD_DIM)
        s = _dot_nt(q[:, sl], kb[:, sl]) * SCALE
        p = jnp.exp(s - jnp.max(s, axis=1, keepdims=True))
        o = _dot(p.astype(BF16), vb[:, sl]) / jnp.sum(p, axis=1, keepdims=True)
        o_ref[:, sl] = o.astype(o_ref.dtype)


def _mem_attention(mq, mk, mv, tq, layer=None):
    B, Tq, _ = mq.shape
    n_mem = mk.shape[-2]
    if layer is None:
        kv_spec = pl.BlockSpec((None, n_mem, MEM_WIDTH), lambda b, qi: (b, 0, 0))
    else:
        kv_spec = pl.BlockSpec((None, None, n_mem, MEM_WIDTH), lambda b, qi: (layer, b, 0, 0))
    return pl.pallas_call(
        _mem_kernel,
        grid=(B, Tq // tq),
        in_specs=[pl.BlockSpec((None, tq, MEM_WIDTH), lambda b, qi: (b, qi, 0)), kv_spec, kv_spec],
        out_specs=pl.BlockSpec((None, tq, MEM_WIDTH), lambda b, qi: (b, qi, 0)),
        out_shape=jax.ShapeDtypeStruct((B, Tq, MEM_WIDTH), BF16),
        compiler_params=_params("parallel", "parallel"),
        name="mem_attention",
    )(mq, mk, mv)


def _top_values(x, n):
    out = []
    for _ in range(n):
        m = jnp.max(x, axis=0, keepdims=True)
        out.append(m)
        x = jnp.where(x == m, -jnp.inf, x)
    return out


def _batcher_pairs(n):
    pairs = []
    p = 1
    while p < n:
        k = p
        while k >= 1:
            for j in range(k % p, n - k, 2 * k):
                for i in range(min(k, n - j - k)):
                    if (i + j) // (2 * p) == (i + j + k) // (2 * p):
                        pairs.append((i + j, i + j + k))
            k //= 2
        p *= 2
    return pairs


def _top_values_sorted(x, n):
    g = x.shape[0] // 8
    cols = []
    for lg in range(x.shape[1] // LANES):
        v = [x[r * 8:(r + 1) * 8, lg * LANES:(lg + 1) * LANES] for r in range(g)]
        for i, j in _batcher_pairs(g):
            v[i], v[j] = jnp.maximum(v[i], v[j]), jnp.minimum(v[i], v[j])
        v.append(jnp.full_like(v[0], -jnp.inf))
        out = []
        for k in range(n):
            m = jnp.max(v[0], axis=0, keepdims=True)
            out.append(m)
            popped = v[0] == m
            for d in range(min(n - 1 - k, g)):
                v[d] = jnp.where(popped, v[d + 1], v[d])
        cols.append(out)
    return [jnp.concatenate([c[k] for c in cols], axis=1) for k in range(n)]


def _peer_select_kernel(q_ref, keys_ref, thr_ref, e1z_ref, s2_ref, e2_ref, cand_sc):
    n_top = PEER_TOPK + 1
    pairs = [(p, q) for p in range(n_top) for q in range(n_top) if (p + 1) * (q + 1) <= n_top]
    for h in range(PEER_HEADS):
        s = []
        for c in range(2):
            col = (2 * h + c) * HEAD_DIM
            s.append(_dot_nt(keys_ref[h, c], q_ref[:, col:col + HEAD_DIM]))
        a = _top_values_sorted(s[0], n_top)
        b = _top_values_sorted(s[1], n_top)
        cand_sc[...] = jnp.full(cand_sc.shape, -jnp.inf, F32)
        for r, (p, q) in enumerate(pairs):
            cand_sc[r:r + 1, :] = a[p] + b[q]
        c = _top_values_sorted(cand_sc[...], n_top)
        z = jnp.ones_like(c[0])
        for kth in range(1, PEER_TOPK):
            z = z + jnp.exp(c[kth] - c[0])
        tau = 0.5 * (c[PEER_TOPK - 1] + c[PEER_TOPK])
        thr_ref[h] = tau - s[0]
        e1z_ref[h] = jnp.exp(s[0] - a[0]) / z
        e2 = jnp.exp(s[1] - b[0])
        for lg in range(s2_ref.shape[1]):
            s2_ref[h, lg] = s[1][:, lg * LANES:(lg + 1) * LANES]
            e2_ref[h, lg] = e2[:, lg * LANES:(lg + 1) * LANES]


def _peer_select(qp, keys, layer, tt):
    T = qp.shape[0]
    tt = _tile(T, tt)
    n_pairs = sum(1 for p in range(1, PEER_TOPK + 2) for q in range(1, PEER_TOPK + 2) if p * q <= PEER_TOPK + 1)
    out = jax.ShapeDtypeStruct((PEER_HEADS, N_KEYS, T), F32)
    spec = pl.BlockSpec((PEER_HEADS, N_KEYS, tt), lambda i: (0, 0, i))
    out_lg = jax.ShapeDtypeStruct((PEER_HEADS, T // LANES, N_KEYS, LANES), F32)
    spec_lg = pl.BlockSpec((PEER_HEADS, tt // LANES, N_KEYS, LANES), lambda i: (0, i, 0, 0))
    return pl.pallas_call(
        _peer_select_kernel,
        grid=(T // tt,),
        in_specs=[pl.BlockSpec((tt, PEER_HEADS * 2 * HEAD_DIM), lambda i: (i, 0)),
                  pl.BlockSpec((None, PEER_HEADS, 2, N_KEYS, HEAD_DIM), lambda i: (layer, 0, 0, 0, 0))],
        out_specs=[spec, spec, spec_lg, spec_lg],
        out_shape=[out, out, out_lg, out_lg],
        scratch_shapes=[pltpu.VMEM((8 * pl.next_power_of_2(-(-n_pairs // 8)), tt), F32)],
        compiler_params=_params("parallel"),
        name="peer_select",
    )(qp, keys)


def _peer_gate(thr_ref, e1z_ref, s2_ref, e2_ref, act_ref, hid_ref, rows, row0, slab, lg, sb):
    ls = slice(lg * LANES, (lg + 1) * LANES)
    js = slice(sb * slab, (sb + 1) * slab)
    gates = [None] * rows
    for h in range(PEER_HEADS):
        s2 = s2_ref[h, lg, js, :]
        e2 = e2_ref[h, lg, js, :]
        for ii in range(rows):
            thr = jnp.broadcast_to(thr_ref[h, row0 + ii:row0 + ii + 1, ls], (slab, LANES))
            e1z = jnp.broadcast_to(e1z_ref[h, row0 + ii:row0 + ii + 1, ls], (slab, LANES))
            g = jnp.where(s2 >= thr, e2 * e1z, 0.0)
            gates[ii] = g if gates[ii] is None else gates[ii] + g
    for ii in range(rows):
        es = slice(ii * N_KEYS + sb * slab, ii * N_KEYS + (sb + 1) * slab)
        a = act_ref[es, ls]
        hid_ref[es, ls] = (0.5 * a * (1.0 + lax.erf(a * (2.0 ** -0.5))) * gates[ii]).astype(BF16)


def _peer_mix_kernel(xt_ref, u_ref, vt_ref, thr_ref, e1z_ref, s2_ref, e2_ref, h_ref, g_ref, b_ref, *refs,
                     rows, ne, alpha, n_ride):
    ride_refs = refs[:n_ride]
    y_ref = refs[n_ride]
    stack_refs = refs[n_ride + 1:-5]
    acc_sc, act0, act1, hid0, hid1 = refs[-5:]
    for j, src in enumerate(ride_refs):
        per = n_ride // len(stack_refs)
        stack_refs[j // per][j % per] = src[...]

    f = pl.program_id(0)
    tt = xt_ref.shape[1]
    slab = GATE_ACC_VREGS * 8 // rows

    @pl.when(f == 0)
    def _():
        acc_sc[...] = jnp.zeros(acc_sc.shape, F32)
        for r in (act0, act1, hid0, hid1):
            r[...] = jnp.zeros(r.shape, r.dtype)

    def stages(act_w, act_r, hid_w, hid_r, row0):
        mt = MXU_TILE
        pieces = []
        m1 = mt
        kp = PEER_MM_TILES * mt
        for n in range(tt // mt):
            for m in range(act_w.shape[0] // m1):
                for k in range(xt_ref.shape[0] // kp):
                    def mm1(m=m, n=n, k=k):
                        d = _dot(u_ref[m * m1:(m + 1) * m1, k * kp:(k + 1) * kp],
                                 xt_ref[k * kp:(k + 1) * kp, n * mt:(n + 1) * mt])
                        if k == 0:
                            act_w[m * m1:(m + 1) * m1, n * mt:(n + 1) * mt] = d
                        else:
                            act_w[m * m1:(m + 1) * m1, n * mt:(n + 1) * mt] += d
                    pieces.append((mm1, PEER_MM_TILES))
        m2 = mt
        mn = min(tt, (PEER_MM_TILES * mt * mt) // hid_r.shape[0])
        for m in range(acc_sc.shape[0] // m2):
            for n in range(tt // mn):
                def mm2(m=m, n=n):
                    acc_sc[m * m2:(m + 1) * m2, n * mn:(n + 1) * mn] += _dot(
                        vt_ref[m * m2:(m + 1) * m2, :], hid_r[:, n * mn:(n + 1) * mn])
                pieces.append((mm2, (hid_r.shape[0] // mt) * (mn // mt)))
        groups = [(lg, sb) for lg in range(tt // LANES) for sb in range(N_KEYS // slab)]
        total = sum(c for _, c in pieces)
        issued = 0
        for gi, (lg, sb) in enumerate(groups):
            while pieces and issued * len(groups) <= gi * total:
                fn, cost = pieces.pop(0)
                fn()
                issued += cost
            _peer_gate(thr_ref, e1z_ref, s2_ref, e2_ref, act_r, hid_w, rows, row0, slab, lg, sb)
        for fn, _ in pieces:
            fn()

    @pl.when(f % 2 == 0)
    def _():
        stages(act0, act1, hid1, hid0, rows % 8)

    @pl.when(f % 2 == 1)
    def _():
        stages(act1, act0, hid0, hid1, 0)

    @pl.when((f >= 2) & ((f - 2) % ne == ne - 1))
    def _():
        y_ref[...] = _layer_norm(alpha * h_ref[...] + acc_sc[...].T, g_ref[...], b_ref[...])
        acc_sc[...] = jnp.zeros(acc_sc.shape, F32)


def _peer_mix(xt, u, vt, sel, layer, h, ln_g, ln_b, alpha, tt, rows, stack=()):
    D, T = xt.shape
    tt = _tile(T, tt)
    ne, eb = vt.shape[1], vt.shape[3]
    assert eb == rows * N_KEYS and ne * eb == u.shape[1] and rows in (4, 8) and ne % 2 == 0
    last = (T // tt) * ne - 1

    def tile_of(f):
        return jnp.clip(f, 0, last) // ne

    def block_of(f):
        return jnp.clip(f, 0, last) % ne

    row_spec = pl.BlockSpec((PEER_HEADS, None, 8, tt), lambda f: (0, block_of(f - 1) * rows // 8, 0, tile_of(f - 1)))
    full_spec = pl.BlockSpec((PEER_HEADS, tt // LANES, N_KEYS, LANES), lambda f: (0, tile_of(f - 1), 0, 0))
    thr, e1z, s2, e2 = sel
    thr = thr.reshape(PEER_HEADS, N_KEYS // 8, 8, T)
    e1z = e1z.reshape(PEER_HEADS, N_KEYS // 8, 8, T)
    ride = [a for group in stack for a in group]
    rb = T // (last + 1)
    assert not ride or (T % (last + 1) == 0 and rb % 8 == 0 and len({len(g) for g in stack}) == 1)
    ride_specs = [pl.BlockSpec((rb, a.shape[1]), lambda f: (jnp.clip(f, 0, last), 0)) for a in ride]
    stack_specs = [pl.BlockSpec((len(g), rb, g[0].shape[1]), lambda f: (0, jnp.clip(f, 0, last), 0)) for g in stack]
    stack_shapes = [jax.ShapeDtypeStruct((len(g), T, g[0].shape[1]), g[0].dtype) for g in stack]
    return pl.pallas_call(
        functools.partial(_peer_mix_kernel, rows=rows, ne=ne, alpha=alpha, n_ride=len(ride)),
        grid=(last + 3,),
        in_specs=[
            pl.BlockSpec((D, tt), lambda f: (0, tile_of(f))),
            pl.BlockSpec((None, eb, D), lambda f: (layer, block_of(f), 0)),
            pl.BlockSpec((None, None, D, eb), lambda f: (layer, block_of(f - 2), 0, 0)),
            row_spec, row_spec, full_spec, full_spec,
            pl.BlockSpec((tt, D), lambda f: (tile_of(f - 2), 0), pipeline_mode=pl.Buffered(1)),
            pl.BlockSpec((None, 1, D), lambda f: (layer, 0, 0)),
            pl.BlockSpec((None, 1, D), lambda f: (layer, 0, 0)),
            *ride_specs,
        ],
        out_specs=[pl.BlockSpec((tt, D), lambda f: (tile_of(f - 2), 0)), *stack_specs],
        out_shape=[jax.ShapeDtypeStruct((T, D), F32), *stack_shapes],
        scratch_shapes=[pltpu.VMEM((D, tt), F32), pltpu.VMEM((eb, tt), F32), pltpu.VMEM((eb, tt), F32),
                        pltpu.VMEM((eb, tt), BF16), pltpu.VMEM((eb, tt), BF16)],
        compiler_params=_params("arbitrary"),
        name="peer_mix",
    )(xt, u, vt, thr, e1z, s2, e2, h, ln_g, ln_b, *ride)


def kernel(x_prompt, x_sample, cache_self_k, cache_self_v, cache_mem_k, cache_mem_v, mem_prompt, w_in, w_o, w_mem_k, w_mem_v, rel_bias_table, diff_lambda, diff_subln_g, ln_g, ln_b, peer_w_q, peer_sub_keys, peer_u, peer_v):
    depth = w_in.shape[0]
    bp, n_prompt, d_model = x_prompt.shape
    bs, n_new, _ = x_sample.shape
    n_past = cache_self_k.shape[2]
    n_mem = mem_prompt.shape[1]
    alpha = (2 * depth) ** 0.25
    S = SELF_WIDTH

    w_in_b = w_in.astype(BF16)
    w_o_b = w_o.astype(BF16)
    w_mk_b = w_mem_k.astype(BF16)
    w_mv_b = w_mem_v.astype(BF16)
    w_pq_b = peer_w_q.astype(BF16)
    keys_b = peer_sub_keys.astype(BF16)
    u_b = peer_u.astype(BF16)
    eb = PEER_ROWS * N_KEYS
    vt_b = jnp.swapaxes(peer_v.reshape(depth, -1, eb, d_model), 2, 3).astype(BF16)
    ln_g4 = ln_g.reshape(depth, 2, 1, d_model)
    ln_b4 = ln_b.reshape(depth, 2, 1, d_model)
    cmk = cache_mem_k.reshape(depth, bs, n_mem, MEM_WIDTH)
    cmv = cache_mem_v.reshape(depth, bs, n_mem, MEM_WIDTH)
    mem2d = mem_prompt.reshape(bp * n_mem, d_model)

    tq_p = _tile(n_prompt, 512)
    tk_s = _tile(n_past, CACHE_TILE)
    nq_p = n_prompt // tq_p
    nk_s = n_past // tk_s
    i32 = jnp.int32
    bias_p = _bias_tiles(rel_bias_table, jnp.arange(nq_p, dtype=i32) * tq_p, jnp.zeros((nq_p,), i32), tq_p, tq_p)
    bias_sc = _bias_tiles(rel_bias_table, jnp.full((nk_s,), n_past, i32), jnp.arange(nk_s, dtype=i32) * tk_s, n_new, tk_s)
    bias_sn = _bias_tiles(rel_bias_table, jnp.full((1,), n_past, i32), jnp.full((1,), n_past, i32), n_new, n_new)

    hp = x_prompt.reshape(bp * n_prompt, d_model)
    hs = x_sample.reshape(bs * n_new, d_model)
    segs = ((0, S), (S, S), (2 * S, S), (3 * S, MEM_WIDTH))
    seg_dt = (BF16, F32, F32, BF16)
    new_k_p, new_v_p, new_mk_p, new_mv_p, new_k_s, new_v_s = [], [], [], [], [], []
    for i in range(depth):
        kind = i % 2
        j = i // 2
        qp, kp, vp, mqp = _linear(hp, w_in_b, i, segs, seg_dt, ROW_TILE)
        qs, ks, vs, mqs = _linear(hs, w_in_b, i, segs, seg_dt, ROW_TILE)
        (mkp,) = _linear(mem2d, w_mk_b, i, ((0, MEM_WIDTH),), (F32,), ROW_TILE)
        (mvp,) = _linear(mem2d, w_mv_b, i, ((0, MEM_WIDTH),), (F32,), ROW_TILE)
        qp3, kp3, vp3 = (a.reshape(bp, n_prompt, S) for a in (qp, kp, vp))
        qs3, ks3, vs3 = (a.reshape(bs, n_new, S) for a in (qs, ks, vs))

        if kind == 0:
            g = diff_subln_g[j].reshape(1, DIFF_VDIM)
            op = _diff_attention(qp3, kp3, vp3, bias_p, diff_lambda[j], g, i, tq_p, tq_p)
            os_ = _diff_attention(qs3, cache_self_k, cache_self_v, bias_sc, diff_lambda[j], g, i, n_new, tk_s,
                                  new=(ks3, vs3, bias_sn))
        else:
            op = _sb_attention(qp3, kp3, vp3, i, tq_p, tq_p)
            os_ = _sb_attention(qs3, cache_self_k, cache_self_v, i, n_new, tk_s, new=(ks3, vs3))

        mop = _mem_attention(mqp.reshape(bp, n_prompt, MEM_WIDTH), mkp.reshape(bp, n_mem, MEM_WIDTH),
                             mvp.reshape(bp, n_mem, MEM_WIDTH), tq_p)
        mos = _mem_attention(mqs.reshape(bs, n_new, MEM_WIDTH), cmk, cmv, n_new, layer=i)

        new_k_p.append(kp)
        new_v_p.append(vp)
        outs = []
        for h_res, o_self, o_mem, is_prompt in ((hp, op, mop, True), (hs, os_, mos, False)):
            T = h_res.shape[0]
            y, yt = _out_proj(o_self.reshape(T, S), o_mem.reshape(T, MEM_WIDTH), w_o_b, i, h_res,
                              ln_g4[:, 0], ln_b4[:, 0], alpha, ROW_TILE)
            (pq,) = _linear(y, w_pq_b, i, ((0, PEER_HEADS * 2 * HEAD_DIM),), (BF16,), ROW_TILE)
            sel = _peer_select(pq, keys_b, i, ROW_TILE)
            stack = (new_k_p, new_v_p) if is_prompt and i == depth - 1 else ()
            y2, *stacked = _peer_mix(yt, u_b, vt_b, sel, i, y, ln_g4[:, 1], ln_b4[:, 1], alpha, ROW_TILE, PEER_ROWS,
                                     stack=stack)
            outs.append(y2)
            if stack:
                k_p_all, v_p_all = (a.reshape(depth, bp, n_prompt, S) for a in stacked)
        hp, hs = outs

        new_mk_p.append(mkp.reshape(bp, n_mem, MEM_HEADS, HEAD_DIM))
        new_mv_p.append(mvp.reshape(bp, n_mem, MEM_HEADS, HEAD_DIM))
        new_k_s.append(ks3)
        new_v_s.append(vs3)

    return (hp.reshape(bp, n_prompt, d_model), hs.reshape(bs, n_new, d_model),
            k_p_all, v_p_all, jnp.stack(new_mk_p), jnp.stack(new_mv_p),
            jnp.stack(new_k_s), jnp.stack(new_v_s))
```

```python
import functools
import math

import jax
import jax.numpy as jnp
from jax import lax
from jax.experimental import pallas as pl
from jax.experimental.pallas import tpu as pltpu

F32 = jnp.float32
BF16 = jnp.bfloat16

HEAD_DIM = 128
SELF_WIDTH = 1536
DIFF_HEADS = 6
DIFF_VDIM = 2 * HEAD_DIM
SB_HEADS = 12
MEM_HEADS = 4
MEM_WIDTH = MEM_HEADS * HEAD_DIM
CHUNK = 64
N_BUCKETS = 32
PEER_HEADS = 8
N_KEYS = 128
PEER_TOPK = 16
LN_EPS = 1e-5
SCALE = HEAD_DIM ** -0.5
NEG = -1e30

V7X_VMEM_LIMIT = 56 * 1024 * 1024
LANES = 128
MXU_TILE = 256
ROW_TILE = 512
CACHE_TILE = 2048
SB_PAIR = 4
PEER_MM_TILES = 4
PEER_ROWS = 4
GATE_ACC_VREGS = 16


def _params(*sem):
    return pltpu.CompilerParams(dimension_semantics=sem, vmem_limit_bytes=V7X_VMEM_LIMIT)


def _tile(n, t):
    t = min(n, t)
    assert n % t == 0, (n, t)
    return t


def _dot(a, b):
    return jnp.dot(a, b, preferred_element_type=F32)


def _dot_nt(a, b):
    return lax.dot_general(a, b, (((1,), (1,)), ((), ())), preferred_element_type=F32)


def _layer_norm(r, g, b):
    mu = jnp.mean(r, axis=-1, keepdims=True)
    d = r - mu
    var = jnp.mean(d * d, axis=-1, keepdims=True)
    return d * lax.rsqrt(var + LN_EPS) * g + b


def _linear_kernel(x_ref, w_ref, *o_refs, segs, nc):
    x = x_ref[...].astype(BF16)
    for o_ref, (start, width) in zip(o_refs, segs):
        for c in range(0, width, nc):
            o_ref[:, c:c + nc] = _dot(x, w_ref[:, start + c:start + c + nc]).astype(o_ref.dtype)


def _linear(x, w, layer, segs, dtypes, tm):
    M, K = x.shape
    N = w.shape[2]
    tm = _tile(M, tm)
    nc = 512
    assert all(wd % nc == 0 for _, wd in segs)
    return pl.pallas_call(
        functools.partial(_linear_kernel, segs=segs, nc=nc),
        grid=(M // tm,),
        in_specs=[
            pl.BlockSpec((tm, K), lambda i: (i, 0)),
            pl.BlockSpec((None, K, N), lambda i: (layer, 0, 0), pipeline_mode=pl.Buffered(1)),
        ],
        out_specs=[pl.BlockSpec((tm, wd), lambda i: (i, 0)) for _, wd in segs],
        out_shape=[jax.ShapeDtypeStruct((M, wd), dt) for (_, wd), dt in zip(segs, dtypes)],
        compiler_params=_params("parallel"),
        name="linear",
    )(x, w)


def _out_proj_kernel(o_ref, mo_ref, w_ref, h_ref, g_ref, b_ref, wq_ref, y_ref, yt_ref, pq_ref, *, alpha):
    acc = _dot(o_ref[...], w_ref[:SELF_WIDTH, :]) + _dot(mo_ref[...], w_ref[SELF_WIDTH:, :])
    y = _layer_norm(alpha * h_ref[...] + acc, g_ref[...], b_ref[...])
    y_ref[...] = y
    yt_ref[...] = y.T.astype(BF16)
    yb = y.astype(BF16)
    for c in range(0, pq_ref.shape[1], ROW_TILE):
        pq_ref[:, c:c + ROW_TILE] = _dot(yb, wq_ref[:, c:c + ROW_TILE]).astype(pq_ref.dtype)


def _out_proj(o, mo, w_o, w_pq, layer, h, ln_g, ln_b, alpha, tm):
    T, D = h.shape
    tm = _tile(T, tm)
    nq = w_pq.shape[2]
    return pl.pallas_call(
        functools.partial(_out_proj_kernel, alpha=alpha),
        grid=(T // tm,),
        in_specs=[
            pl.BlockSpec((tm, SELF_WIDTH), lambda i: (i, 0)),
            pl.BlockSpec((tm, MEM_WIDTH), lambda i: (i, 0)),
            pl.BlockSpec((None, SELF_WIDTH + MEM_WIDTH, D), lambda i: (layer, 0, 0), pipeline_mode=pl.Buffered(1)),
            pl.BlockSpec((tm, D), lambda i: (i, 0)),
            pl.BlockSpec((None, 1, D), lambda i: (layer, 0, 0)),
            pl.BlockSpec((None, 1, D), lambda i: (layer, 0, 0)),
            pl.BlockSpec((None, D, nq), lambda i: (layer, 0, 0), pipeline_mode=pl.Buffered(1)),
        ],
        out_specs=[pl.BlockSpec((tm, D), lambda i: (i, 0)), pl.BlockSpec((D, tm), lambda i: (0, i)),
                   pl.BlockSpec((tm, nq), lambda i: (i, 0))],
        out_shape=[jax.ShapeDtypeStruct((T, D), F32), jax.ShapeDtypeStruct((D, T), BF16),
                   jax.ShapeDtypeStruct((T, nq), BF16)],
        compiler_params=_params("parallel"),
        name="out_proj_ln",
    )(o, mo, w_o, h, ln_g, ln_b, w_pq)


def _bias_kernel(q0_ref, k0_ref, tab_ref, o_ref, *, tq, tk):
    h = pl.program_id(0)
    t = pl.program_id(1)
    q_pos = q0_ref[t] + lax.broadcasted_iota(jnp.int32, (tq, tk), 0)
    k_pos = k0_ref[t] + lax.broadcasted_iota(jnp.int32, (tq, tk), 1)
    rel = k_pos - q_pos
    n = jnp.abs(rel)
    large = jnp.full((tq, tk), 8, jnp.int32)
    for thr in (12, 16, 23, 32, 46, 64, 91):
        large = large + (n >= thr).astype(jnp.int32)
    bucket = jnp.where(rel > 0, N_BUCKETS // 2, 0) + jnp.where(n < 8, n, large)
    bias = jnp.zeros((tq, tk), F32)
    for b in range(N_BUCKETS):
        bias = jnp.where(bucket == b, tab_ref[b, h], bias)
    visible = (k_pos // CHUNK) <= (q_pos // CHUNK)
    o_ref[...] = jnp.where(visible, bias, NEG)


def _bias_tiles(rel_table, q0, k0, tq, tk):
    n = q0.shape[0]
    return pl.pallas_call(
        functools.partial(_bias_kernel, tq=tq, tk=tk),
        grid_spec=pltpu.PrefetchScalarGridSpec(
            num_scalar_prefetch=2,
            grid=(DIFF_HEADS, n),
            in_specs=[pl.BlockSpec(memory_space=pltpu.SMEM)],
            out_specs=pl.BlockSpec((None, None, tq, tk), lambda h, t, *_: (h, t, 0, 0)),
        ),
        out_shape=jax.ShapeDtypeStruct((DIFF_HEADS, n, tq, tk), F32),
        compiler_params=_params("parallel", "parallel"),
        name="bias_tiles",
    )(q0, k0, rel_table)


def _diff_step(q_ref, k_ref, v_ref, bias_ref, m_sc, l_sc, acc_sc):
    tq, tk = bias_ref.shape
    cw = min(tk, LANES)
    kb = k_ref[...].astype(BF16)
    v_ext = jnp.concatenate([v_ref[...].astype(BF16), jnp.ones((tk, LANES), BF16)], axis=1)
    live = [dict(), dict()]

    def scores(c):
        cs = slice(c * HEAD_DIM, (c + 1) * HEAD_DIM)
        live[c]["s"] = _dot_nt(q_ref[:, cs], kb[:, cs]) * SCALE + bias_ref[...]

    def probs(c):
        s = live[c].pop("s")
        m_prev = m_sc[c]
        m_new = jnp.maximum(m_prev, jnp.broadcast_to(jnp.max(s, axis=1, keepdims=True), (tq, LANES)))
        live[c]["alpha"] = jnp.exp(m_prev - m_new)
        m_sc[c] = m_new
        p = [jnp.exp(s[:, j:j + cw] - m_new[:, :cw]) for j in range(0, tk, cw)]
        live[c]["p"] = jnp.concatenate(p, axis=1).astype(BF16)

    def values(c):
        live[c]["pv"] = _dot(live[c].pop("p"), v_ext)

    def update(c):
        alpha = live[c].pop("alpha")
        pv = live[c].pop("pv")
        acc_sc[c] = jnp.concatenate([alpha, alpha], axis=1) * acc_sc[c] + pv[:, :DIFF_VDIM]
        l_sc[c] = alpha * l_sc[c] + pv[:, DIFF_VDIM:]

    stages = (scores, probs, values, update)
    for t in range(2 + len(stages) - 1):
        for s in reversed(range(len(stages))):
            if 0 <= t - s < 2:
                stages[s](t - s)


def _diff_kernel(qi_ref, kk_ref, fin_ref, *refs, has_new, lam_init):
    if has_new:
        lam_ref, g_ref, q_ref, k_ref, v_ref, bias_ref, kn_ref, vn_ref, biasn_ref, o_ref, m_sc, l_sc, acc_sc = refs
    else:
        lam_ref, g_ref, q_ref, k_ref, v_ref, bias_ref, o_ref, m_sc, l_sc, acc_sc = refs
    t = pl.program_id(2)
    kk = kk_ref[t]

    @pl.when(kk == 0)
    def _():
        m_sc[...] = jnp.full(m_sc.shape, NEG, F32)
        l_sc[...] = jnp.zeros(l_sc.shape, F32)
        acc_sc[...] = jnp.zeros(acc_sc.shape, F32)

    if has_new:
        @pl.when(kk == 0)
        def _():
            _diff_step(q_ref, kn_ref, vn_ref, biasn_ref, m_sc, l_sc, acc_sc)

        @pl.when(kk > 0)
        def _():
            _diff_step(q_ref, k_ref, v_ref, bias_ref, m_sc, l_sc, acc_sc)
    else:
        _diff_step(q_ref, k_ref, v_ref, bias_ref, m_sc, l_sc, acc_sc)

    @pl.when(fin_ref[t] == 1)
    def _():
        lp = lam_ref[...]
        lam = (jnp.exp(jnp.sum(lp[0:1] * lp[1:2], axis=1, keepdims=True))
               - jnp.exp(jnp.sum(lp[2:3] * lp[3:4], axis=1, keepdims=True)) + lam_init)
        l0 = jnp.concatenate([l_sc[0], l_sc[0]], axis=1)
        l1 = jnp.concatenate([l_sc[1], l_sc[1]], axis=1)
        o = acc_sc[0] / l0 - lam * (acc_sc[1] / l1)
        o = o * lax.rsqrt(jnp.mean(o * o, axis=1, keepdims=True) + LN_EPS) * g_ref[...]
        o_ref[...] = (o * (1.0 - lam_init)).astype(o_ref.dtype)


def _attention_steps(nq, nk, causal):
    steps = [(qi, kk) for qi in range(nq) for kk in range(qi + 1 if causal else nk)]
    qi_t = jnp.array([s[0] for s in steps], jnp.int32)
    kk_t = jnp.array([s[1] for s in steps], jnp.int32)
    fin_t = jnp.array([int(kk == (qi if causal else nk - 1)) for qi, kk in steps], jnp.int32)
    return qi_t, kk_t, fin_t


def _diff_attention(q, k, v, bias, lam_vec, subln_g, layer, tq, tk, new=None):
    B, Tq, _ = q.shape
    lam_init = 0.8 - 0.6 * math.exp(-0.3 * layer)
    has_new = new is not None
    if has_new:
        nkc = k.shape[2] // tk
        tables = _attention_steps(Tq // tq, nkc + 1, causal=False)
        kv_spec = pl.BlockSpec((None, None, tk, DIFF_VDIM),
                               lambda b, h, t, qi, kk, fin: (layer, b, jnp.clip(nkc - kk[t], 0, nkc - 1), h))
        bias_spec = pl.BlockSpec((None, None, tq, tk),
                                 lambda b, h, t, qi, kk, fin: (h, jnp.clip(nkc - kk[t], 0, nkc - 1), 0, 0))
        tn = new[0].shape[1]
        new_specs = [pl.BlockSpec((None, tn, DIFF_VDIM), lambda b, h, t, qi, kk, fin: (b, 0, h)),
                     pl.BlockSpec((None, tn, DIFF_VDIM), lambda b, h, t, qi, kk, fin: (b, 0, h)),
                     pl.BlockSpec((None, None, tq, tn), lambda b, h, t, qi, kk, fin: (h, 0, 0, 0))]
        new_args = list(new)
    else:
        tables = _attention_steps(Tq // tq, k.shape[1] // tk, causal=True)
        kv_spec = pl.BlockSpec((None, tk, DIFF_VDIM), lambda b, h, t, qi, kk, fin: (b, qi[t] - kk[t], h))
        bias_spec = pl.BlockSpec((None, None, tq, tk), lambda b, h, t, qi, kk, fin: (h, kk[t], 0, 0))
        new_specs, new_args = [], []
    return pl.pallas_call(
        functools.partial(_diff_kernel, has_new=has_new, lam_init=lam_init),
        grid_spec=pltpu.PrefetchScalarGridSpec(
            num_scalar_prefetch=3,
            grid=(B, DIFF_HEADS, tables[0].shape[0]),
            in_specs=[
                pl.BlockSpec((4, HEAD_DIM), lambda b, h, t, qi, kk, fin: (0, 0)),
                pl.BlockSpec((1, DIFF_VDIM), lambda b, h, t, qi, kk, fin: (0, 0)),
                pl.BlockSpec((None, tq, DIFF_VDIM), lambda b, h, t, qi, kk, fin: (b, qi[t], h)),
                kv_spec, kv_spec, bias_spec, *new_specs,
            ],
            out_specs=pl.BlockSpec((None, tq, DIFF_VDIM), lambda b, h, t, qi, kk, fin: (b, qi[t], h)),
            scratch_shapes=[pltpu.VMEM((2, tq, LANES), F32), pltpu.VMEM((2, tq, LANES), F32),
                            pltpu.VMEM((2, tq, DIFF_VDIM), F32)],
        ),
        out_shape=jax.ShapeDtypeStruct((B, Tq, SELF_WIDTH), BF16),
        compiler_params=_params("parallel", "parallel", "arbitrary"),
        name="diff_attention",
    )(*tables, lam_vec, subln_g, q, k, v, bias, *new_args)


def _sb_step(q_ref, k_ref, v_ref, q_pos0, k_pos0, acc_sc, r_sc, cw, masked):
    tq = q_ref.shape[0]
    tk = k_ref.shape[0]
    cw = min(cw, tk)
    later = (lax.broadcasted_iota(jnp.int32, (cw, cw), 0) > lax.broadcasted_iota(jnp.int32, (cw, cw), 1))
    later = jnp.where(later, 1.0, 0.0).astype(BF16)
    units = [(c0, hh) for c0 in range(tk - cw, -1, -cw) for hh in range(SB_PAIR)]
    live = [dict() for _ in units]

    def scores(u):
        c0, hh = units[u]
        hs = slice(hh * HEAD_DIM, (hh + 1) * HEAD_DIM)
        live[u]["z"] = _dot_nt(q_ref[:, hs], k_ref[c0:c0 + cw, hs].astype(BF16)) * SCALE

    def logs(u):
        c0, _ = units[u]
        z = live[u].pop("z")
        log_keep = -(jnp.maximum(z, 0.0) + jnp.log(1.0 + jnp.exp(-jnp.abs(z))))
        live[u]["log_beta"] = log_keep + z
        if masked:
            q_pos = q_pos0 + lax.broadcasted_iota(jnp.int32, (tq, cw), 0)
            before = (k_pos0 + c0 + lax.broadcasted_iota(jnp.int32, (tq, cw), 1)) < q_pos
            log_keep = jnp.where(before, log_keep, 0.0)
            live[u]["before"] = before
        live[u]["log_keep"] = log_keep

    def cumsum(u):
        live[u]["cum"] = _dot(live[u]["log_keep"].astype(BF16), later)

    def weights(u):
        _, hh = units[u]
        a = jnp.exp(live[u].pop("log_beta") + (live[u].pop("cum") + r_sc[hh]))
        if masked:
            a = jnp.where(live[u].pop("before"), a, 0.0)
        live[u]["a"] = a.astype(BF16)
        r_sc[hh] += jnp.sum(live[u].pop("log_keep"), axis=1, keepdims=True)

    def values(u):
        c0, hh = units[u]
        hs = slice(hh * HEAD_DIM, (hh + 1) * HEAD_DIM)
        acc_sc[hh] += _dot(live[u].pop("a"), v_ref[c0:c0 + cw, hs].astype(BF16))

    stages = (scores, logs, cumsum, weights, values)
    for t in range(len(units) + len(stages) - 1):
        for s in reversed(range(len(stages))):
            if 0 <= t - s < len(units):
                stages[s](t - s)


def _sb_kernel(qi_ref, kk_ref, fin_ref, *refs, has_new, tq, tk, q_off, nkc, cw):
    if has_new:
        q_ref, k_ref, v_ref, kn_ref, vn_ref, o_ref, acc_sc, r_sc = refs
    else:
        q_ref, k_ref, v_ref, o_ref, acc_sc, r_sc = refs
    t = pl.program_id(2)
    qi = qi_ref[t]
    kk = kk_ref[t]

    @pl.when(kk == 0)
    def _():
        acc_sc[...] = jnp.zeros(acc_sc.shape, F32)
        r_sc[...] = jnp.zeros(r_sc.shape, F32)

    if has_new:
        @pl.when(kk == 0)
        def _():
            _sb_step(q_ref, kn_ref, vn_ref, q_off, q_off, acc_sc, r_sc, cw, True)

        @pl.when(kk > 0)
        def _():
            _sb_step(q_ref, k_ref, v_ref, q_off, (nkc - kk) * tk, acc_sc, r_sc, cw, False)
    else:
        @pl.when(kk == 0)
        def _():
            _sb_step(q_ref, k_ref, v_ref, qi * tq, qi * tk, acc_sc, r_sc, cw, True)

        @pl.when(kk > 0)
        def _():
            _sb_step(q_ref, k_ref, v_ref, qi * tq, (qi - kk) * tk, acc_sc, r_sc, cw, False)

    @pl.when(fin_ref[t] == 1)
    def _():
        for hh in range(SB_PAIR):
            o_ref[:, hh * HEAD_DIM:(hh + 1) * HEAD_DIM] = acc_sc[hh].astype(o_ref.dtype)


def _sb_attention(q, k, v, layer, tq, tk, new=None):
    B, Tq, _ = q.shape
    has_new = new is not None
    hw = SB_PAIR * HEAD_DIM
    if has_new:
        nkc = k.shape[2] // tk
        q_off = k.shape[2]
        tables = _attention_steps(Tq // tq, nkc + 1, causal=False)
        kv_spec = pl.BlockSpec((None, None, tk, hw),
                               lambda b, h, t, qi, kk, fin: (layer, b, jnp.clip(nkc - kk[t], 0, nkc - 1), h))
        tn = new[0].shape[1]
        new_specs = [pl.BlockSpec((None, tn, hw), lambda b, h, t, qi, kk, fin: (b, 0, h))] * 2
        new_args = list(new)
    else:
        assert tq == tk
        nkc = k.shape[1] // tk
        q_off = 0
        tables = _attention_steps(Tq // tq, nkc, causal=True)
        kv_spec = pl.BlockSpec((None, tk, hw), lambda b, h, t, qi, kk, fin: (b, qi[t] - kk[t], h))
        new_specs, new_args = [], []
    return pl.pallas_call(
        functools.partial(_sb_kernel, has_new=has_new, tq=tq, tk=tk, q_off=q_off, nkc=nkc, cw=MXU_TILE),
        grid_spec=pltpu.PrefetchScalarGridSpec(
            num_scalar_prefetch=3,
            grid=(B, SB_HEADS // SB_PAIR, tables[0].shape[0]),
            in_specs=[pl.BlockSpec((None, tq, hw), lambda b, h, t, qi, kk, fin: (b, qi[t], h)),
                      kv_spec, kv_spec, *new_specs],
            out_specs=pl.BlockSpec((None, tq, hw), lambda b, h, t, qi, kk, fin: (b, qi[t], h)),
            scratch_shapes=[pltpu.VMEM((SB_PAIR, tq, HEAD_DIM), F32), pltpu.VMEM((SB_PAIR, tq, 1), F32)],
        ),
        out_shape=jax.ShapeDtypeStruct((B, Tq, SELF_WIDTH), BF16),
        compiler_params=_params("parallel", "parallel", "arbitrary"),
        name="sb_attention",
    )(*tables, q, k, v, *new_args)


def _mem_kernel(q_ref, k_ref, v_ref, o_ref):
    q = q_ref[...]
    kb = k_ref[...].astype(BF16)
    vb = v_ref[...].astype(BF16)
    for h in range(MEM_HEADS):
        sl = slice(h * HEAD_DIM, (h + 1) * HEAD_DIM)
        s = _dot_nt(q[:, sl], kb[:, sl]) * SCALE
        p = jnp.exp(s - jnp.max(s, axis=1, keepdims=True))
        o = _dot(p.astype(BF16), vb[:, sl]) / jnp.sum(p, axis=1, keepdims=True)
        o_ref[:, sl] = o.astype(o_ref.dtype)


def _mem_attention(mq, mk, mv, tq, layer=None):
    B, Tq, _ = mq.shape
    n_mem = mk.shape[-2]
    if layer is None:
        kv_spec = pl.BlockSpec((None, n_mem, MEM_WIDTH), lambda b, qi: (b, 0, 0))
    else:
        kv_spec = pl.BlockSpec((None, None, n_mem, MEM_WIDTH), lambda b, qi: (layer, b, 0, 0))
    return pl.pallas_call(
        _mem_kernel,
        grid=(B, Tq // tq),
        in_specs=[pl.BlockSpec((None, tq, MEM_WIDTH), lambda b, qi: (b, qi, 0)), kv_spec, kv_spec],
        out_specs=pl.BlockSpec((None, tq, MEM_WIDTH), lambda b, qi: (b, qi, 0)),
        out_shape=jax.ShapeDtypeStruct((B, Tq, MEM_WIDTH), BF16),
        compiler_params=_params("parallel", "parallel"),
        name="mem_attention",
    )(mq, mk, mv)


def _top_values(x, n):
    out = []
    for _ in range(n):
        m = jnp.max(x, axis=0, keepdims=True)
        out.append(m)
        x = jnp.where(x == m, -jnp.inf, x)
    return out


def _batcher_pairs(n):
    pairs = []
    p = 1
    while p < n:
        k = p
        while k >= 1:
            for j in range(k % p, n - k, 2 * k):
                for i in range(min(k, n - j - k)):
                    if (i + j) // (2 * p) == (i + j + k) // (2 * p):
                        pairs.append((i + j, i + j + k))
            k //= 2
        p *= 2
    return pairs


def _top_values_sorted(x, n):
    g = x.shape[0] // 8
    cols = []
    for lg in range(x.shape[1] // LANES):
        v = [x[r * 8:(r + 1) * 8, lg * LANES:(lg + 1) * LANES] for r in range(g)]
        for i, j in _batcher_pairs(g):
            v[i], v[j] = jnp.maximum(v[i], v[j]), jnp.minimum(v[i], v[j])
        v.append(jnp.full_like(v[0], -jnp.inf))
        out = []
        for k in range(n):
            m = jnp.max(v[0], axis=0, keepdims=True)
            out.append(m)
            popped = v[0] == m
            for d in range(min(n - 1 - k, g)):
                v[d] = jnp.where(popped, v[d + 1], v[d])
        cols.append(out)
    return [jnp.concatenate([c[k] for c in cols], axis=1) for k in range(n)]


def _peer_select_kernel(q_ref, keys_ref, thr_ref, e1z_ref, s2_ref, e2_ref, cand_sc):
    n_top = PEER_TOPK + 1
    pairs = [(p, q) for p in range(n_top) for q in range(n_top) if (p + 1) * (q + 1) <= n_top]
    for h in range(PEER_HEADS):
        s = []
        for c in range(2):
            col = (2 * h + c) * HEAD_DIM
            s.append(_dot_nt(keys_ref[h, c], q_ref[:, col:col + HEAD_DIM]))
        a = _top_values_sorted(s[0], n_top)
        b = _top_values_sorted(s[1], n_top)
        cand_sc[...] = jnp.full(cand_sc.shape, -jnp.inf, F32)
        for r, (p, q) in enumerate(pairs):
            cand_sc[r:r + 1, :] = a[p] + b[q]
        c = _top_values_sorted(cand_sc[...], n_top)
        z = jnp.ones_like(c[0])
        for kth in range(1, PEER_TOPK):
            z = z + jnp.exp(c[kth] - c[0])
        tau = 0.5 * (c[PEER_TOPK - 1] + c[PEER_TOPK])
        thr_ref[h] = tau - s[0]
        e1z_ref[h] = jnp.exp(s[0] - a[0]) / z
        e2 = jnp.exp(s[1] - b[0])
        for lg in range(s2_ref.shape[1]):
            s2_ref[h, lg] = s[1][:, lg * LANES:(lg + 1) * LANES]
            e2_ref[h, lg] = e2[:, lg * LANES:(lg + 1) * LANES]


def _peer_select(qp, keys, layer, tt):
    T = qp.shape[0]
    tt = _tile(T, tt)
    n_pairs = sum(1 for p in range(1, PEER_TOPK + 2) for q in range(1, PEER_TOPK + 2) if p * q <= PEER_TOPK + 1)
    out = jax.ShapeDtypeStruct((PEER_HEADS, N_KEYS, T), F32)
    spec = pl.BlockSpec((PEER_HEADS, N_KEYS, tt), lambda i: (0, 0, i))
    out_lg = jax.ShapeDtypeStruct((PEER_HEADS, T // LANES, N_KEYS, LANES), F32)
    spec_lg = pl.BlockSpec((PEER_HEADS, tt // LANES, N_KEYS, LANES), lambda i: (0, i, 0, 0))
    return pl.pallas_call(
        _peer_select_kernel,
        grid=(T // tt,),
        in_specs=[pl.BlockSpec((tt, PEER_HEADS * 2 * HEAD_DIM), lambda i: (i, 0)),
                  pl.BlockSpec((None, PEER_HEADS, 2, N_KEYS, HEAD_DIM), lambda i: (layer, 0, 0, 0, 0))],
        out_specs=[spec, spec, spec_lg, spec_lg],
        out_shape=[out, out, out_lg, out_lg],
        scratch_shapes=[pltpu.VMEM((8 * pl.next_power_of_2(-(-n_pairs // 8)), tt), F32)],
        compiler_params=_params("parallel"),
        name="peer_select",
    )(qp, keys)


def _peer_gate(thr_ref, e1z_ref, s2_ref, e2_ref, act_ref, hid_ref, rows, row0, slab, lg, sb):
    ls = slice(lg * LANES, (lg + 1) * LANES)
    js = slice(sb * slab, (sb + 1) * slab)
    gates = [None] * rows
    for h in range(PEER_HEADS):
        s2 = s2_ref[h, lg, js, :]
        e2 = e2_ref[h, lg, js, :]
        for ii in range(rows):
            thr = jnp.broadcast_to(thr_ref[h, row0 + ii:row0 + ii + 1, ls], (slab, LANES))
            e1z = jnp.broadcast_to(e1z_ref[h, row0 + ii:row0 + ii + 1, ls], (slab, LANES))
            g = jnp.where(s2 >= thr, e2 * e1z, 0.0)
            gates[ii] = g if gates[ii] is None else gates[ii] + g
    for ii in range(rows):
        es = slice(ii * N_KEYS + sb * slab, ii * N_KEYS + (sb + 1) * slab)
        a = act_ref[es, ls]
        hid_ref[es, ls] = (0.5 * a * (1.0 + lax.erf(a * (2.0 ** -0.5))) * gates[ii]).astype(BF16)


def _peer_mix_kernel(xt_ref, u_ref, vt_ref, thr_ref, e1z_ref, s2_ref, e2_ref, h_ref, g_ref, b_ref, *refs,
                     rows, ne, alpha, n_ride):
    ride_refs = refs[:n_ride]
    y_ref = refs[n_ride]
    stack_refs = refs[n_ride + 1:-5]
    acc_sc, act0, act1, hid0, hid1 = refs[-5:]
    for j, src in enumerate(ride_refs):
        per = n_ride // len(stack_refs)
        stack_refs[j // per][j % per] = src[...]

    f = pl.program_id(0)
    tt = xt_ref.shape[1]
    slab = GATE_ACC_VREGS * 8 // rows

    @pl.when(f == 0)
    def _():
        acc_sc[...] = jnp.zeros(acc_sc.shape, F32)
        for r in (act0, act1, hid0, hid1):
            r[...] = jnp.zeros(r.shape, r.dtype)

    def stages(act_w, act_r, hid_w, hid_r, row0):
        mt = MXU_TILE
        pieces = []
        m1 = mt
        kp = PEER_MM_TILES * mt
        for n in range(tt // mt):
            for m in range(act_w.shape[0] // m1):
                for k in range(xt_ref.shape[0] // kp):
                    def mm1(m=m, n=n, k=k):
                        d = _dot(u_ref[m * m1:(m + 1) * m1, k * kp:(k + 1) * kp],
                                 xt_ref[k * kp:(k + 1) * kp, n * mt:(n + 1) * mt])
                        if k == 0:
                            act_w[m * m1:(m + 1) * m1, n * mt:(n + 1) * mt] = d
                        else:
                            act_w[m * m1:(m + 1) * m1, n * mt:(n + 1) * mt] += d
                    pieces.append((mm1, PEER_MM_TILES))
        m2 = mt
        mn = min(tt, (PEER_MM_TILES * mt * mt) // hid_r.shape[0])
        for m in range(acc_sc.shape[0] // m2):
            for n in range(tt // mn):
                def mm2(m=m, n=n):
                    acc_sc[m * m2:(m + 1) * m2, n * mn:(n + 1) * mn] += _dot(
                        vt_ref[m * m2:(m + 1) * m2, :], hid_r[:, n * mn:(n + 1) * mn])
                pieces.append((mm2, (hid_r.shape[0] // mt) * (mn // mt)))
        groups = [(lg, sb) for lg in range(tt // LANES) for sb in range(N_KEYS // slab)]
        total = sum(c for _, c in pieces)
        issued = 0
        for gi, (lg, sb) in enumerate(groups):
            while pieces and issued * len(groups) <= gi * total:
                fn, cost = pieces.pop(0)
                fn()
                issued += cost
            _peer_gate(thr_ref, e1z_ref, s2_ref, e2_ref, act_r, hid_w, rows, row0, slab, lg, sb)
        for fn, _ in pieces:
            fn()

    @pl.when(f % 2 == 0)
    def _():
        stages(act0, act1, hid1, hid0, rows % 8)

    @pl.when(f % 2 == 1)
    def _():
        stages(act1, act0, hid0, hid1, 0)

    @pl.when((f >= 2) & ((f - 2) % ne == ne - 1))
    def _():
        y_ref[...] = _layer_norm(alpha * h_ref[...] + acc_sc[...].T, g_ref[...], b_ref[...])
        acc_sc[...] = jnp.zeros(acc_sc.shape, F32)


def _peer_mix(xt, u, vt, sel, layer, h, ln_g, ln_b, alpha, tt, rows, stack=()):
    D, T = xt.shape
    tt = _tile(T, tt)
    ne, eb = vt.shape[1], vt.shape[3]
    assert eb == rows * N_KEYS and ne * eb == u.shape[1] and rows in (4, 8) and ne % 2 == 0
    last = (T // tt) * ne - 1

    def tile_of(f):
        return jnp.clip(f, 0, last) // ne

    def block_of(f):
        return jnp.clip(f, 0, last) % ne

    row_spec = pl.BlockSpec((PEER_HEADS, None, 8, tt), lambda f: (0, block_of(f - 1) * rows // 8, 0, tile_of(f - 1)))
    full_spec = pl.BlockSpec((PEER_HEADS, tt // LANES, N_KEYS, LANES), lambda f: (0, tile_of(f - 1), 0, 0))
    thr, e1z, s2, e2 = sel
    thr = thr.reshape(PEER_HEADS, N_KEYS // 8, 8, T)
    e1z = e1z.reshape(PEER_HEADS, N_KEYS // 8, 8, T)
    ride = [a for group in stack for a in group]
    rb = T // (last + 1)
    assert not ride or (T % (last + 1) == 0 and rb % 8 == 0 and len({len(g) for g in stack}) == 1)
    ride_specs = [pl.BlockSpec((rb, a.shape[1]), lambda f: (jnp.clip(f, 0, last), 0)) for a in ride]
    stack_specs = [pl.BlockSpec((len(g), rb, g[0].shape[1]), lambda f: (0, jnp.clip(f, 0, last), 0)) for g in stack]
    stack_shapes = [jax.ShapeDtypeStruct((len(g), T, g[0].shape[1]), g[0].dtype) for g in stack]
    return pl.pallas_call(
        functools.partial(_peer_mix_kernel, rows=rows, ne=ne, alpha=alpha, n_ride=len(ride)),
        grid=(last + 3,),
        in_specs=[
            pl.BlockSpec((D, tt), lambda f: (0, tile_of(f))),
            pl.BlockSpec((None, eb, D), lambda f: (layer, block_of(f), 0)),
            pl.BlockSpec((None, None, D, eb), lambda f: (layer, block_of(f - 2), 0, 0)),
            row_spec, row_spec, full_spec, full_spec,
            pl.BlockSpec((tt, D), lambda f: (tile_of(f - 2), 0), pipeline_mode=pl.Buffered(1)),
            pl.BlockSpec((None, 1, D), lambda f: (layer, 0, 0)),
            pl.BlockSpec((None, 1, D), lambda f: (layer, 0, 0)),
            *ride_specs,
        ],
        out_specs=[pl.BlockSpec((tt, D), lambda f: (tile_of(f - 2), 0)), *stack_specs],
        out_shape=[jax.ShapeDtypeStruct((T, D), F32), *stack_shapes],
        scratch_shapes=[pltpu.VMEM((D, tt), F32), pltpu.VMEM((eb, tt), F32), pltpu.VMEM((eb, tt), F32),
                        pltpu.VMEM((eb, tt), BF16), pltpu.VMEM((eb, tt), BF16)],
        compiler_params=_params("arbitrary"),
        name="peer_mix",
    )(xt, u, vt, thr, e1z, s2, e2, h, ln_g, ln_b, *ride)


def kernel(x_prompt, x_sample, cache_self_k, cache_self_v, cache_mem_k, cache_mem_v, mem_prompt, w_in, w_o, w_mem_k, w_mem_v, rel_bias_table, diff_lambda, diff_subln_g, ln_g, ln_b, peer_w_q, peer_sub_keys, peer_u, peer_v):
    depth = w_in.shape[0]
    bp, n_prompt, d_model = x_prompt.shape
    bs, n_new, _ = x_sample.shape
    n_past = cache_self_k.shape[2]
    n_mem = mem_prompt.shape[1]
    alpha = (2 * depth) ** 0.25
    S = SELF_WIDTH

    w_in_b = w_in.astype(BF16)
    w_o_b = w_o.astype(BF16)
    w_mk_b = w_mem_k.astype(BF16)
    w_mv_b = w_mem_v.astype(BF16)
    w_pq_b = peer_w_q.astype(BF16)
    keys_b = peer_sub_keys.astype(BF16)
    u_b = peer_u.astype(BF16)
    eb = PEER_ROWS * N_KEYS
    vt_b = jnp.swapaxes(peer_v.reshape(depth, -1, eb, d_model), 2, 3).astype(BF16)
    ln_g4 = ln_g.reshape(depth, 2, 1, d_model)
    ln_b4 = ln_b.reshape(depth, 2, 1, d_model)
    cmk = cache_mem_k.reshape(depth, bs, n_mem, MEM_WIDTH)
    cmv = cache_mem_v.reshape(depth, bs, n_mem, MEM_WIDTH)
    mem2d = mem_prompt.reshape(bp * n_mem, d_model)

    tq_p = _tile(n_prompt, 512)
    tk_s = _tile(n_past, CACHE_TILE)
    nq_p = n_prompt // tq_p
    nk_s = n_past // tk_s
    i32 = jnp.int32
    bias_p = _bias_tiles(rel_bias_table, jnp.arange(nq_p, dtype=i32) * tq_p, jnp.zeros((nq_p,), i32), tq_p, tq_p)
    bias_sc = _bias_tiles(rel_bias_table, jnp.full((nk_s,), n_past, i32), jnp.arange(nk_s, dtype=i32) * tk_s, n_new, tk_s)
    bias_sn = _bias_tiles(rel_bias_table, jnp.full((1,), n_past, i32), jnp.full((1,), n_past, i32), n_new, n_new)

    hp = x_prompt.reshape(bp * n_prompt, d_model)
    hs = x_sample.reshape(bs * n_new, d_model)
    segs = ((0, S), (S, S), (2 * S, S), (3 * S, MEM_WIDTH))
    seg_dt = (BF16, F32, F32, BF16)
    new_k_p, new_v_p, new_mk_p, new_mv_p, new_k_s, new_v_s = [], [], [], [], [], []
    for i in range(depth):
        kind = i % 2
        j = i // 2
        qp, kp, vp, mqp = _linear(hp, w_in_b, i, segs, seg_dt, ROW_TILE)
        qs, ks, vs, mqs = _linear(hs, w_in_b, i, segs, seg_dt, ROW_TILE)
        (mkp,) = _linear(mem2d, w_mk_b, i, ((0, MEM_WIDTH),), (F32,), ROW_TILE)
        (mvp,) = _linear(mem2d, w_mv_b, i, ((0, MEM_WIDTH),), (F32,), ROW_TILE)
        qp3, kp3, vp3 = (a.reshape(bp, n_prompt, S) for a in (qp, kp, vp))
        qs3, ks3, vs3 = (a.reshape(bs, n_new, S) for a in (qs, ks, vs))

        if kind == 0:
            g = diff_subln_g[j].reshape(1, DIFF_VDIM)
            op = _diff_attention(qp3, kp3, vp3, bias_p, diff_lambda[j], g, i, tq_p, tq_p)
            os_ = _diff_attention(qs3, cache_self_k, cache_self_v, bias_sc, diff_lambda[j], g, i, n_new, tk_s,
                                  new=(ks3, vs3, bias_sn))
        else:
            op = _sb_attention(qp3, kp3, vp3, i, tq_p, tq_p)
            os_ = _sb_attention(qs3, cache_self_k, cache_self_v, i, n_new, tk_s, new=(ks3, vs3))

        mop = _mem_attention(mqp.reshape(bp, n_prompt, MEM_WIDTH), mkp.reshape(bp, n_mem, MEM_WIDTH),
                             mvp.reshape(bp, n_mem, MEM_WIDTH), tq_p)
        mos = _mem_attention(mqs.reshape(bs, n_new, MEM_WIDTH), cmk, cmv, n_new, layer=i)

        new_k_p.append(kp)
        new_v_p.append(vp)
        outs = []
        for h_res, o_self, o_mem, is_prompt in ((hp, op, mop, True), (hs, os_, mos, False)):
            T = h_res.shape[0]
            y, yt, pq = _out_proj(o_self.reshape(T, S), o_mem.reshape(T, MEM_WIDTH), w_o_b, w_pq_b, i, h_res,
                                  ln_g4[:, 0], ln_b4[:, 0], alpha, ROW_TILE)
            sel = _peer_select(pq, keys_b, i, ROW_TILE)
            stack = (new_k_p, new_v_p) if is_prompt and i == depth - 1 else ()
            y2, *stacked = _peer_mix(yt, u_b, vt_b, sel, i, y, ln_g4[:, 1], ln_b4[:, 1], alpha, ROW_TILE, PEER_ROWS,
                                     stack=stack)
            outs.append(y2)
            if stack:
                k_p_all, v_p_all = (a.reshape(depth, bp, n_prompt, S) for a in stacked)
        hp, hs = outs

        new_mk_p.append(mkp.reshape(bp, n_mem, MEM_HEADS, HEAD_DIM))
        new_mv_p.append(mvp.reshape(bp, n_mem, MEM_HEADS, HEAD_DIM))
        new_k_s.append(ks3)
        new_v_s.append(vs3)

    return (hp.reshape(bp, n_prompt, d_model), hs.reshape(bs, n_new, d_model),
            k_p_all, v_p_all, jnp.stack(new_mk_p), jnp.stack(new_mv_p),
            jnp.stack(new_k_s), jnp.stack(new_v_s))
```
